```python
import math
import jax, jax.numpy as jnp
from jax import lax
import numpy as np


D_MODEL = 2048
BATCH = 2
SEQ = 8192
DEPTH = 4

D_MIX = D_MODEL
S5_WIDTH = D_MIX // 4
S5_GROUP = 16
S5_GROUPS = S5_WIDTH // S5_GROUP
S5_STATE = 64
S5_DT_MIN = 1e-3
S5_DT_MAX = 1e-1
S5_EIG_CLIP = -1e-4
GDN_HEAD_DIM = 128
GDN_WIDTH = D_MIX // 2
GDN_HEADS = GDN_WIDTH // GDN_HEAD_DIM
GDN_CHUNK = 64
LRU_WIDTH = D_MIX - S5_WIDTH - GDN_WIDTH
LRU_BLOCKS = 8
LRU_BLOCK = LRU_WIDTH // LRU_BLOCKS
LRU_C = 8.0
CONV_WIDTH = 4
D_FF = ((8 * D_MODEL // 3 + 255) // 256) * 256
IN_SPLITS = (S5_WIDTH, 3 * GDN_WIDTH, GDN_WIDTH, GDN_HEADS, GDN_HEADS, LRU_WIDTH, LRU_WIDTH)
D_IN_PROJ = sum(IN_SPLITS)
LN_EPS = 1e-5
RMS_EPS = 1e-6
DEEPNORM_ALPHA = (2.0 * DEPTH) ** 0.25
DEEPNORM_BETA = (8.0 * DEPTH) ** -0.25
MACARON_WEIGHT = 0.5

kernel_name = 'hybrid_s5_gdn_rglru_macaron_deepnorm'


def layer_norm(x, g, b):
    xf = x.astype(jnp.float32)
    mu = jnp.mean(xf, axis=-1, keepdims=True)
    xc = xf - mu
    var = jnp.mean(xc * xc, axis=-1, keepdims=True)
    return (xc * lax.rsqrt(var + LN_EPS) * g + b).astype(x.dtype)


def swiglu(x, w_gate, w_up, w_down):
    return (jax.nn.silu(x @ w_gate) * (x @ w_up)) @ w_down


def causal_depthwise_conv(x, w):
    k_width, seq = w.shape[0], x.shape[1]
    xp = jnp.pad(x, ((0, 0), (k_width - 1, 0), (0, 0)))
    y = xp[:, 0:seq] * w[0]
    for k in range(1, k_width):
        y = y + xp[:, k:k + seq] * w[k]
    return y


def linear_recurrence_combine(left, right):
    a_l, b_l = left
    a_r, b_r = right
    return a_l * a_r, a_r * b_l + b_r


def s5_mixer(u, lam_re, lam_im, b_re, b_im, c_re, c_im, d, log_step, w_glu, b_glu):
    bsz, seq, _ = u.shape
    uf = u.astype(jnp.float32).reshape(bsz, seq, S5_GROUPS, S5_GROUP)
    lam = lax.complex(jnp.minimum(lam_re.astype(jnp.float32), S5_EIG_CLIP), lam_im.astype(jnp.float32))
    dt = jnp.exp(log_step.astype(jnp.float32))[:, None]
    lam_bar = jnp.exp(lam * dt)
    b_mat = lax.complex(b_re.astype(jnp.float32), b_im.astype(jnp.float32))
    b_bar = ((lam_bar - 1.0) / lam)[..., None] * b_mat
    bu = jnp.einsum('blgp,gnp->blgn', uf.astype(jnp.complex64), b_bar)
    a = jnp.broadcast_to(lam_bar, bu.shape)
    _, h = lax.associative_scan(linear_recurrence_combine, (a, bu), axis=1)
    c_mat = lax.complex(c_re.astype(jnp.float32), c_im.astype(jnp.float32))
    y = jnp.einsum('blgn,gpn->blgp', h, c_mat).real + d.astype(jnp.float32).reshape(S5_GROUPS, S5_GROUP) * uf
    y = jax.nn.gelu(y.reshape(bsz, seq, S5_WIDTH))
    return y * jax.nn.sigmoid(y @ w_glu + b_glu)


def l2_normalize(x):
    return x * lax.rsqrt(jnp.sum(x * x, axis=-1, keepdims=True) + RMS_EPS)


def chunk_gated_delta_rule(q, k, v, g, beta):
    bsz, seq, heads, dk = q.shape
    dv = v.shape[-1]
    n_chunks = seq // GDN_CHUNK

    def to_chunks(t):
        return t.reshape(bsz, n_chunks, GDN_CHUNK, heads, -1).transpose(1, 0, 3, 2, 4)

    q, k, v = to_chunks(q), to_chunks(k), to_chunks(v)
    beta_c = beta.reshape(bsz, n_chunks, GDN_CHUNK, heads).transpose(1, 0, 3, 2)
    gc = jnp.cumsum(g.reshape(bsz, n_chunks, GDN_CHUNK, heads).transpose(1, 0, 3, 2), axis=-1)
    causal = jnp.tril(jnp.ones((GDN_CHUNK, GDN_CHUNK), dtype=bool))
    strict = jnp.tril(jnp.ones((GDN_CHUNK, GDN_CHUNK), dtype=bool), -1)
    decay = jnp.exp(jnp.where(causal, gc[..., :, None] - gc[..., None, :], -jnp.inf))
    k_beta = k * beta_c[..., None]
    v_beta = v * beta_c[..., None]
    m = jnp.where(strict, jnp.einsum('nbhid,nbhjd->nbhij', k_beta, k) * decay, 0.0)
    t_mat = jnp.eye(GDN_CHUNK, dtype=jnp.float32) + m
    rhs = jnp.concatenate([v_beta, k_beta * jnp.exp(gc)[..., None]], axis=-1)
    sol = lax.linalg.triangular_solve(t_mat, rhs, left_side=True, lower=True, unit_diagonal=True)
    u_val, w_val = sol[..., :dv], sol[..., dv:]
    qk = jnp.where(causal, jnp.einsum('nbhid,nbhjd->nbhij', q, k) * decay, 0.0)
    q_dec = q * jnp.exp(gc)[..., None]
    k_dec = k * jnp.exp(gc[..., -1:] - gc)[..., None]
    g_last = jnp.exp(gc[..., -1])

    def step(state, xs):
        u_i, w_i, qd_i, qk_i, kd_i, gl_i = xs
        v_new = u_i - jnp.einsum('bhck,bhkv->bhcv', w_i, state)
        o_i = jnp.einsum('bhck,bhkv->bhcv', qd_i, state) + jnp.einsum('bhij,bhjv->bhiv', qk_i, v_new)
        state = state * gl_i[..., None, None] + jnp.einsum('bhck,bhcv->bhkv', kd_i, v_new)
        return state, o_i

    state0 = jnp.zeros((bsz, heads, dk, dv), jnp.float32)
    _, o = lax.scan(step, state0, (u_val, w_val, q_dec, qk, k_dec, g_last))
    return o.transpose(1, 0, 3, 2, 4).reshape(bsz, seq, heads, dv)


def gdn_mixer(qkv, z, a_logit, b_logit, conv_w, a_log, dt_bias, norm_g):
    bsz, seq, _ = qkv.shape
    qkv = jax.nn.silu(causal_depthwise_conv(qkv, conv_w)).astype(jnp.float32)
    q, k, v = jnp.split(qkv, 3, axis=-1)
    heads_shape = (bsz, seq, GDN_HEADS, GDN_HEAD_DIM)
    q = l2_normalize(q.reshape(heads_shape)) * (GDN_HEAD_DIM ** -0.5)
    k = l2_normalize(k.reshape(heads_shape))
    v = v.reshape(heads_shape)
    beta = jax.nn.sigmoid(b_logit.astype(jnp.float32))
    g = -jnp.exp(a_log.astype(jnp.float32)) * jax.nn.softplus(a_logit.astype(jnp.float32) + dt_bias)
    o = chunk_gated_delta_rule(q, k, v, g, beta)
    o = o * lax.rsqrt(jnp.mean(o * o, axis=-1, keepdims=True) + RMS_EPS) * norm_g
    o = o * jax.nn.silu(z.astype(jnp.float32).reshape(heads_shape))
    return o.reshape(bsz, seq, GDN_WIDTH)


def rglru_mixer(xb, gate, conv_w, conv_b, w_a, b_a, w_x, b_x, lam):
    bsz, seq, _ = xb.shape
    xc = (causal_depthwise_conv(xb, conv_w) + conv_b).astype(jnp.float32)
    xh = xc.reshape(bsz, seq, LRU_BLOCKS, LRU_BLOCK)
    r = jax.nn.sigmoid(jnp.einsum('blhi,hij->blhj', xh, w_a).reshape(bsz, seq, LRU_WIDTH) + b_a)
    i = jax.nn.sigmoid(jnp.einsum('blhi,hij->blhj', xh, w_x).reshape(bsz, seq, LRU_WIDTH) + b_x)
    log_a = -LRU_C * r * jax.nn.softplus(-lam.astype(jnp.float32))
    a = jnp.exp(log_a)
    bt = jnp.sqrt(-jnp.expm1(2.0 * log_a)) * (i * xc)
    _, h = lax.associative_scan(linear_recurrence_combine, (a, bt), axis=1)
    return h * jax.nn.gelu(gate.astype(jnp.float32))


def setup_inputs(seed: int = 0) -> dict:
    key = jax.random.key(seed)
    keys = jax.random.split(key, 64)
    counter = [0]

    def nk():
        counter[0] += 1
        return keys[counter[0]]

    def normal(shape, scale):
        return jax.random.normal(nk(), shape, jnp.float32) * scale

    def uniform(shape, lo, hi):
        return jax.random.uniform(nk(), shape, jnp.float32, lo, hi)

    L = DEPTH
    inp = {}
    inp['x'] = normal((BATCH, SEQ, D_MODEL), 1.0)
    inp['ffn1_w_gate'] = normal((L, D_MODEL, D_FF), D_MODEL ** -0.5)
    inp['ffn1_w_up'] = normal((L, D_MODEL, D_FF), D_MODEL ** -0.5)
    inp['ffn1_w_down'] = normal((L, D_FF, D_MODEL), DEEPNORM_BETA * D_FF ** -0.5)
    inp['ln1_g'] = 1.0 + normal((L, D_MODEL), 0.02)
    inp['ln1_b'] = normal((L, D_MODEL), 0.02)
    inp['w_in'] = normal((L, D_MODEL, D_IN_PROJ), D_MODEL ** -0.5)
    n_idx = jnp.arange(S5_STATE, dtype=jnp.float32)
    inp['s5_lambda_re'] = -0.5 + normal((L, S5_GROUPS, S5_STATE), 0.01)
    inp['s5_lambda_im'] = math.pi * n_idx + normal((L, S5_GROUPS, S5_STATE), 0.01)
    inp['s5_b_re'] = normal((L, S5_GROUPS, S5_STATE, S5_GROUP), (2.0 * S5_GROUP) ** -0.5)
    inp['s5_b_im'] = normal((L, S5_GROUPS, S5_STATE, S5_GROUP), (2.0 * S5_GROUP) ** -0.5)
    inp['s5_c_re'] = normal((L, S5_GROUPS, S5_GROUP, S5_STATE), S5_STATE ** -0.5)
    inp['s5_c_im'] = normal((L, S5_GROUPS, S5_GROUP, S5_STATE), S5_STATE ** -0.5)
    inp['s5_d'] = normal((L, S5_WIDTH), 1.0)
    inp['s5_log_step'] = uniform((L, S5_GROUPS), math.log(S5_DT_MIN), math.log(S5_DT_MAX))
    inp['s5_w_glu'] = normal((L, S5_WIDTH, S5_WIDTH), S5_WIDTH ** -0.5)
    inp['s5_b_glu'] = normal((L, S5_WIDTH), 0.02)
    inp['gdn_conv_w'] = normal((L, CONV_WIDTH, 3 * GDN_WIDTH), CONV_WIDTH ** -0.5)
    inp['gdn_a_log'] = jnp.log(uniform((L, GDN_HEADS), 1.0, 16.0))
    dt = jnp.exp(uniform((L, GDN_HEADS), math.log(1e-3), math.log(1e-1)))
    inp['gdn_dt_bias'] = dt + jnp.log(-jnp.expm1(-dt))
    inp['gdn_norm_g'] = 1.0 + normal((L, GDN_HEAD_DIM), 0.02)
    inp['lru_conv_w'] = normal((L, CONV_WIDTH, LRU_WIDTH), CONV_WIDTH ** -0.5)
    inp['lru_conv_b'] = normal((L, LRU_WIDTH), 0.02)
    inp['lru_w_a'] = normal((L, LRU_BLOCKS, LRU_BLOCK, LRU_BLOCK), LRU_BLOCK ** -0.5)
    inp['lru_b_a'] = normal((L, LRU_WIDTH), 0.02)
    inp['lru_w_x'] = normal((L, LRU_BLOCKS, LRU_BLOCK, LRU_BLOCK), LRU_BLOCK ** -0.5)
    inp['lru_b_x'] = normal((L, LRU_WIDTH), 0.02)
    a0 = uniform((L, LRU_WIDTH), 0.9, 0.999) ** (1.0 / LRU_C)
    inp['lru_lambda'] = jnp.log(a0) - jnp.log1p(-a0)
    inp['w_out'] = normal((L, D_MIX, D_MODEL), DEEPNORM_BETA * D_MIX ** -0.5)
    inp['ln2_g'] = 1.0 + normal((L, D_MODEL), 0.02)
    inp['ln2_b'] = normal((L, D_MODEL), 0.02)
    inp['ffn2_w_gate'] = normal((L, D_MODEL, D_FF), D_MODEL ** -0.5)
    inp['ffn2_w_up'] = normal((L, D_MODEL, D_FF), D_MODEL ** -0.5)
    inp['ffn2_w_down'] = normal((L, D_FF, D_MODEL), DEEPNORM_BETA * D_FF ** -0.5)
    inp['ln3_g'] = 1.0 + normal((L, D_MODEL), 0.02)
    inp['ln3_b'] = normal((L, D_MODEL), 0.02)
    return inp


def reference(x, ffn1_w_gate, ffn1_w_up, ffn1_w_down, ln1_g, ln1_b, w_in,
              s5_lambda_re, s5_lambda_im, s5_b_re, s5_b_im, s5_c_re, s5_c_im, s5_d, s5_log_step,
              s5_w_glu, s5_b_glu, gdn_conv_w, gdn_a_log, gdn_dt_bias, gdn_norm_g,
              lru_conv_w, lru_conv_b, lru_w_a, lru_b_a, lru_w_x, lru_b_x, lru_lambda,
              w_out, ln2_g, ln2_b, ffn2_w_gate, ffn2_w_up, ffn2_w_down, ln3_g, ln3_b):
    split_points = []
    acc = 0
    for size in IN_SPLITS[:-1]:
        acc += size
        split_points.append(acc)
    for l in range(DEPTH):
        x = layer_norm(DEEPNORM_ALPHA * x + MACARON_WEIGHT * swiglu(x, ffn1_w_gate[l], ffn1_w_up[l], ffn1_w_down[l]),
                       ln1_g[l], ln1_b[l])
        proj = x @ w_in[l]
        s5_u, gdn_qkv, gdn_z, gdn_a, gdn_b, lru_x, lru_gate = jnp.split(proj, split_points, axis=-1)
        y_s5 = s5_mixer(s5_u, s5_lambda_re[l], s5_lambda_im[l], s5_b_re[l], s5_b_im[l], s5_c_re[l], s5_c_im[l],
                        s5_d[l], s5_log_step[l], s5_w_glu[l], s5_b_glu[l])
        y_gdn = gdn_mixer(gdn_qkv, gdn_z, gdn_a, gdn_b, gdn_conv_w[l], gdn_a_log[l], gdn_dt_bias[l], gdn_norm_g[l])
        y_lru = rglru_mixer(lru_x, lru_gate, lru_conv_w[l], lru_conv_b[l], lru_w_a[l], lru_b_a[l],
                            lru_w_x[l], lru_b_x[l], lru_lambda[l])
        mixed = jnp.concatenate([y_s5, y_gdn, y_lru], axis=-1) @ w_out[l]
        x = layer_norm(DEEPNORM_ALPHA * x + mixed.astype(x.dtype), ln2_g[l], ln2_b[l])
        x = layer_norm(DEEPNORM_ALPHA * x + MACARON_WEIGHT * swiglu(x, ffn2_w_gate[l], ffn2_w_up[l], ffn2_w_down[l]),
                       ln3_g[l], ln3_b[l])
    return x
```

```python
import functools
import math

import jax
import jax.numpy as jnp
from jax import lax
from jax.experimental import pallas as pl
from jax.experimental.pallas import tpu as pltpu

F32 = jnp.float32
BF16 = jnp.bfloat16

S5_GROUP = 16
S5_STATE = 64
S5_EIG_CLIP = -1e-4
S5_CHUNK = 16
GDN_HEAD_DIM = 128
GDN_CHUNK = 64
GDN_HEADS_PER_STEP = 4
LRU_C = 8.0
CONV_WIDTH = 4
LN_EPS = 1e-5
RMS_EPS = 1e-6
MACARON_WEIGHT = 0.5

LANES = 128
SUBLANES = 8
VMEM_LIMIT_BYTES = 56 * 1024 * 1024

FFN_TM = 512
FFN_TF = 512
PROJ_TM = 512
PROJ_TN = 1024
OUT_TM = 256
LRU_TL = 512
GDN_TB = 128
PROJ_BLOCK = 512


def _cparams(sem):
    return pltpu.CompilerParams(dimension_semantics=sem, vmem_limit_bytes=VMEM_LIMIT_BYTES)


def _split2(x):
    hi = x.astype(BF16)
    lo = (x - hi.astype(F32)).astype(BF16)
    return hi, lo


def _mm(a, b):
    return jnp.dot(a, b, preferred_element_type=F32)


def _mm_nt(a, b):
    return lax.dot_general(a, b, (((1,), (1,)), ((), ())), preferred_element_type=F32)


def _mm_tn(a, b):
    return lax.dot_general(a, b, (((0,), (0,)), ((), ())), preferred_element_type=F32)


def _mm3(a, b):
    ah, al = _split2(a)
    bh, bl = _split2(b)
    return _mm(ah, bh) + (_mm(al, bh) + _mm(ah, bl))


def _mm3_nt(a, b):
    ah, al = _split2(a)
    bh, bl = _split2(b)
    return _mm_nt(ah, bh) + (_mm_nt(al, bh) + _mm_nt(ah, bl))


def _mm3_pre(a, bh, bl):
    ah, al = _split2(a)
    return _mm(ah, bh) + (_mm(al, bh) + _mm(ah, bl))


def _layer_norm(y, g, b):
    mu = jnp.mean(y, axis=-1, keepdims=True)
    yc = y - mu
    var = jnp.mean(yc * yc, axis=-1, keepdims=True)
    return yc * lax.rsqrt(var + LN_EPS) * g + b


def _sigmoid(x):
    return 1.0 / (1.0 + jnp.exp(-x))


def _silu(x):
    return x * _sigmoid(x)


def _gelu_tanh(x):
    c = math.sqrt(2.0 / math.pi)
    return 0.5 * x * (1.0 + jnp.tanh(c * (x + 0.044715 * (x * x * x))))


def _softplus(x):
    return jnp.maximum(x, 0.0) + jnp.log1p(jnp.exp(-jnp.abs(x)))


def _shift_rows(x, s, row):
    return jnp.where(row >= s, pltpu.roll(x, s, 0), 0.0)


def _causal_conv(x, halo, cw):
    rows = x.shape[0]
    xp = jnp.concatenate([halo, x], axis=0)
    base = SUBLANES - (CONV_WIDTH - 1)
    y = cw[0:1] * xp[base:base + rows]
    for k in range(1, CONV_WIDTH):
        y = y + cw[k:k + 1] * xp[base + k:base + k + rows]
    return y


def _ffn_kernel(alpha, x_ref, wg_ref, wu_ref, wd_ref, g_ref, b_ref, o_ref, xb_ref):
    j = pl.program_id(1)

    @pl.when(j == 0)
    def _():
        xb_ref[...] = x_ref[...].astype(BF16)
        o_ref[...] = jnp.zeros_like(o_ref)

    xb = xb_ref[...]
    gate = _mm(xb, wg_ref[...])
    up = _mm(xb, wu_ref[...])
    h = (_silu(gate) * up).astype(BF16)
    o_ref[...] += _mm(h, wd_ref[...])

    @pl.when(j == pl.num_programs(1) - 1)
    def _():
        y = alpha * x_ref[...] + MACARON_WEIGHT * o_ref[...]
        o_ref[...] = _layer_norm(y, g_ref[...], b_ref[...])


def _ffn_ln(x, wg, wu, wd, g, b, layer, alpha):
    t, d = x.shape
    f = wg.shape[-1]
    tm, tf = min(FFN_TM, t), min(FFN_TF, f)
    return pl.pallas_call(
        functools.partial(_ffn_kernel, alpha),
        grid=(t // tm, f // tf),
        in_specs=[
            pl.BlockSpec((tm, d), lambda i, j: (i, 0)),
            pl.BlockSpec((None, d, tf), lambda i, j: (layer, 0, j)),
            pl.BlockSpec((None, d, tf), lambda i, j: (layer, 0, j)),
            pl.BlockSpec((None, tf, d), lambda i, j: (layer, j, 0)),
            pl.BlockSpec((None, 1, d), lambda i, j: (layer, 0, 0)),
            pl.BlockSpec((None, 1, d), lambda i, j: (layer, 0, 0)),
        ],
        out_specs=pl.BlockSpec((tm, d), lambda i, j: (i, 0)),
        out_shape=jax.ShapeDtypeStruct((t, d), F32),
        scratch_shapes=[pltpu.VMEM((tm, d), BF16)],
        compiler_params=_cparams(("parallel", "arbitrary")),
        name="ffn_ln",
    )(x, wg, wu, wd, g, b)


def _proj_kernel(x_ref, w_ref, o_ref, xb_ref):
    @pl.when(pl.program_id(1) == 0)
    def _():
        xb_ref[...] = x_ref[...].astype(BF16)

    o_ref[...] = _mm(xb_ref[...], w_ref[...])


def _in_proj(x, w, layer):
    t, d = x.shape
    n = w.shape[-1]
    tm, tn = min(PROJ_TM, t), PROJ_TN
    return pl.pallas_call(
        _proj_kernel,
        grid=(t // tm, n // tn),
        in_specs=[
            pl.BlockSpec((tm, d), lambda i, j: (i, 0)),
            pl.BlockSpec((None, d, tn), lambda i, j: (layer, 0, j)),
        ],
        out_specs=pl.BlockSpec((tm, tn), lambda i, j: (i, j)),
        out_shape=jax.ShapeDtypeStruct((t, n), F32),
        scratch_shapes=[pltpu.VMEM((tm, d), BF16)],
        compiler_params=_cparams(("parallel", "arbitrary")),
        name="in_proj",
    )(x, w)


def _s5_kernel(seg, u_ref, th_ref, tl_ref, gh_ref, gl_ref, eh_ref, el_ref, p1_ref, p2_ref, d_ref, o_ref):
    u = u_ref[...]
    uh, ul = _split2(u)
    y = d_ref[...] * u + (_mm(uh, th_ref[...]) + (_mm(ul, th_ref[...]) + _mm(uh, tl_ref[...])))
    z = _mm(uh, gh_ref[...]) + (_mm(ul, gh_ref[...]) + _mm(uh, gl_ref[...]))
    row = lax.broadcasted_iota(jnp.int32, z.shape, 0) & (seg - 1)
    half = z.shape[1] // 2
    p1 = p1_ref[...]
    p2 = p2_ref[...]
    s, k = 1, 0
    while s < seg:
        zs = _shift_rows(z, s, row)
        z = z + (zs * p1[k:k + 1] + pltpu.roll(zs, half, 1) * p2[k:k + 1])
        s *= 2
        k += 1
    h_in = _shift_rows(z, 1, row)
    hh, hl = _split2(h_in)
    y = y + (_mm(hh, eh_ref[...]) + (_mm(hl, eh_ref[...]) + _mm(hh, el_ref[...])))
    o_ref[...] = y


def _s5_core(u_g, mats, seg):
    groups, rows, width = u_g.shape
    th, tl, gh, gl, eh, el, p1, p2, dflat = mats
    n2 = gh.shape[-1]

    def spec(a, b):
        return pl.BlockSpec((None, a, b), lambda g: (g, 0, 0))

    return pl.pallas_call(
        functools.partial(_s5_kernel, seg),
        grid=(groups,),
        in_specs=[spec(rows, width), spec(width, width), spec(width, width), spec(width, n2), spec(width, n2),
                  spec(n2, width), spec(n2, width), spec(p1.shape[1], n2), spec(p2.shape[1], n2), spec(1, width)],
        out_specs=spec(rows, width),
        out_shape=jax.ShapeDtypeStruct((groups, rows, width), F32),
        compiler_params=_cparams(("parallel",)),
        name="s5_core",
    )(u_g, th, tl, gh, gl, eh, el, p1, p2, dflat)


def _s5_matrices(lam_re, lam_im, b_re, b_im, c_re, c_im, d, log_step, seg):
    groups, n = lam_re.shape
    p = S5_GROUP
    cs = S5_CHUNK
    lam = lax.complex(jnp.minimum(lam_re.astype(F32), S5_EIG_CLIP), lam_im.astype(F32))
    dt = jnp.exp(log_step.astype(F32))[:, None]
    zdt = lam * dt
    lam_bar = jnp.exp(zdt)
    b_bar = ((lam_bar - 1.0) / lam)[..., None] * lax.complex(b_re.astype(F32), b_im.astype(F32))
    c_mat = lax.complex(c_re.astype(F32), c_im.astype(F32))
    taus = jnp.arange(cs + 1, dtype=F32)
    pw = jnp.exp(taus[None, :, None] * zdt[:, None, :])
    kmat = jnp.einsum('gpn,gtn,gnq->gtpq', c_mat, pw[:, :cs], b_bar).real
    ii = jnp.arange(cs)[:, None]
    jj = jnp.arange(cs)[None, :]
    lag = jnp.clip(jj - ii, 0, cs - 1)
    tmat = jnp.where((jj >= ii)[None, :, :, None, None], kmat[:, lag], 0.0)
    tmat = tmat.transpose(0, 1, 4, 2, 3).reshape(groups, cs * p, cs * p)
    gin = pw[:, cs - 1 - jnp.arange(cs)][:, :, :, None] * b_bar[:, None]
    gin = gin.transpose(0, 1, 3, 2).reshape(groups, cs * p, n)
    gmat = jnp.concatenate([gin.real, gin.imag], axis=-1)
    eo = c_mat[:, None] * pw[:, 1:cs + 1][:, :, None, :]
    eo = eo.transpose(0, 3, 1, 2).reshape(groups, n, cs * p)
    emat = jnp.concatenate([eo.real, -eo.imag], axis=1)
    nsteps = max(1, int(math.log2(seg)))
    ks = (cs * (2.0 ** jnp.arange(nsteps, dtype=F32)))
    mult = jnp.exp(ks[None, :, None] * zdt[:, None, :])
    p1 = jnp.concatenate([mult.real, mult.real], axis=-1)
    p2 = jnp.concatenate([-mult.imag, mult.imag], axis=-1)
    pad = (-nsteps) % SUBLANES
    p1 = jnp.pad(p1, ((0, 0), (0, pad), (0, 0)))
    p2 = jnp.pad(p2, ((0, 0), (0, pad), (0, 0)))
    dflat = jnp.tile(d.astype(F32).reshape(groups, 1, p), (1, 1, cs))
    th, tl = _split2(tmat)
    gh, gl = _split2(gmat)
    eh, el = _split2(emat)
    return th, tl, gh, gl, eh, el, p1, p2, dflat


def _s5_mix(u, mats, bsz, seq):
    t, width = u.shape
    groups = width // S5_GROUP
    nchunk = seq // S5_CHUNK
    u_g = u.reshape(bsz, nchunk, S5_CHUNK, groups, S5_GROUP).transpose(3, 0, 1, 2, 4)
    u_g = u_g.reshape(groups, bsz * nchunk, S5_CHUNK * S5_GROUP)
    y_g = _s5_core(u_g, mats, nchunk)
    y = y_g.reshape(groups, bsz, nchunk, S5_CHUNK, S5_GROUP).transpose(1, 2, 3, 0, 4)
    return y.reshape(t, width)


def _lru_kernel(x_ref, gate_ref, cw_ref, cb_ref, wh_ref, wl_ref, bias_ref, lam_ref, o_ref, halo_ref, h_ref):
    @pl.when(pl.program_id(1) == 0)
    def _():
        halo_ref[...] = jnp.zeros_like(halo_ref)
        h_ref[...] = jnp.zeros_like(h_ref)

    x = x_ref[...]
    rows, width = x.shape
    xc = _causal_conv(x, halo_ref[...], cw_ref[...]) + cb_ref[...]
    halo_ref[...] = x[rows - SUBLANES:rows]
    gates = _mm3_pre(xc, wh_ref[...], wl_ref[...]) + bias_ref[...]
    r = _sigmoid(gates[:, :width])
    i = _sigmoid(gates[:, width:])
    log_a = (-LRU_C) * r * _softplus(-lam_ref[...])
    a = jnp.exp(log_a)
    b = jnp.sqrt(-jnp.tanh(log_a) * (a * a + 1.0)) * (i * xc)
    row = lax.broadcasted_iota(jnp.int32, a.shape, 0)
    s = 1
    while s < rows:
        m = row >= s
        a_s = jnp.where(m, pltpu.roll(a, s, 0), 1.0)
        b_s = jnp.where(m, pltpu.roll(b, s, 0), 0.0)
        b = a * b_s + b
        a = a * a_s
        s *= 2
    h = a * h_ref[0:1] + b
    h_ref[0:1] = h[rows - 1:rows]
    o_ref[...] = h * _gelu_tanh(gate_ref[...])


def _lru_mix(proj, x_blk, gate_blk, cw, cb, wh, wl, bias, lam, layer, bsz, seq):
    width = cw.shape[-1]
    tl = min(LRU_TL, seq)
    nt = seq // tl

    def vec(rows):
        return pl.BlockSpec((None, rows, width), lambda b, c: (layer, 0, 0))

    return pl.pallas_call(
        _lru_kernel,
        grid=(bsz, nt),
        in_specs=[
            pl.BlockSpec((tl, width), lambda b, c: (b * nt + c, x_blk)),
            pl.BlockSpec((tl, width), lambda b, c: (b * nt + c, gate_blk)),
            vec(CONV_WIDTH), vec(1),
            pl.BlockSpec((None, width, 2 * width), lambda b, c: (layer, 0, 0)),
            pl.BlockSpec((None, width, 2 * width), lambda b, c: (layer, 0, 0)),
            pl.BlockSpec((None, 1, 2 * width), lambda b, c: (layer, 0, 0)),
            vec(1),
        ],
        out_specs=pl.BlockSpec((tl, width), lambda b, c: (b * nt + c, 0)),
        out_shape=jax.ShapeDtypeStruct((bsz * seq, width), F32),
        scratch_shapes=[pltpu.VMEM((SUBLANES, width), F32), pltpu.VMEM((SUBLANES, width), F32)],
        compiler_params=_cparams(("parallel", "arbitrary")),
        name="rglru",
    )(proj, proj, cw, cb, wh, wl, bias, lam)


def _tri_inverse(m, eye, masks):
    x = eye - jnp.where(masks[0], m, 0.0)
    for mk in masks[1:]:
        x = x - _mm3(_mm3(x, jnp.where(mk, m, 0.0)), x)
    return x


def _gdn_kernel(q_ref, k_ref, v_ref, z_ref, ab_ref, cq_ref, ck_ref, cv_ref, alog_ref, dtb_ref, ng_ref, o_ref,
                s_ref, hq_ref, hk_ref, hv_ref):
    @pl.when(pl.program_id(2) == 0)
    def _():
        s_ref[...] = jnp.zeros_like(s_ref)
        hq_ref[...] = jnp.zeros_like(hq_ref)
        hk_ref[...] = jnp.zeros_like(hk_ref)
        hv_ref[...] = jnp.zeros_like(hv_ref)

    rows = q_ref.shape[0]
    cs = GDN_CHUNK
    hd = GDN_HEAD_DIM
    nheads = GDN_HEADS_PER_STEP

    def conv_silu(x_ref, halo_ref, cw_ref):
        x = x_ref[...]
        y = _causal_conv(x, halo_ref[...], cw_ref[...])
        halo_ref[...] = x[rows - SUBLANES:rows]
        return _silu(y)

    q = conv_silu(q_ref, hq_ref, cq_ref)
    k = conv_silu(k_ref, hk_ref, ck_ref)
    v = conv_silu(v_ref, hv_ref, cv_ref)
    z = z_ref[...]

    ab = ab_ref[...]
    g_all = -jnp.exp(alog_ref[...]) * _softplus(ab + dtb_ref[...])
    beta_all = _sigmoid(ab)
    crow = lax.broadcasted_iota(jnp.int32, g_all.shape, 0) & (cs - 1)
    gc = g_all
    s = 1
    while s < cs:
        gc = gc + _shift_rows(gc, s, crow)
        s *= 2
    gc_t = gc.T

    ii = lax.broadcasted_iota(jnp.int32, (cs, cs), 0)
    jj = lax.broadcasted_iota(jnp.int32, (cs, cs), 1)
    lower = ii >= jj
    strict = ii > jj
    eye = jnp.where(ii == jj, 1.0, 0.0).astype(F32)
    masks = [strict & ((ii >> 1) == (jj >> 1))]
    lvl = 1
    while (1 << lvl) < cs:
        masks.append(strict & ((ii >> (lvl + 1)) == (jj >> (lvl + 1))) & ((ii >> lvl) != (jj >> lvl)))
        lvl += 1

    ng = ng_ref[...]
    for h in range(nheads):
        hs = slice(h * hd, (h + 1) * hd)
        qh, kh, vh, zh = q[:, hs], k[:, hs], v[:, hs], z[:, hs]
        qh = qh * (lax.rsqrt(jnp.sum(qh * qh, axis=-1, keepdims=True) + RMS_EPS) * (hd ** -0.5))
        kh = kh * lax.rsqrt(jnp.sum(kh * kh, axis=-1, keepdims=True) + RMS_EPS)
        beta = beta_all[:, nheads + h:nheads + h + 1]
        gcol = gc[:, h:h + 1]
        for c in range(rows // cs):
            rs = slice(c * cs, (c + 1) * cs)
            qc, kc, vc = qh[rs], kh[rs], vh[rs]
            bc = beta[rs]
            gcc = gcol[rs]
            grow = gc_t[h:h + 1, rs]
            dec = jnp.exp(jnp.where(lower, gcc - grow, -jnp.inf))
            kb = kc * bc
            vb = vc * bc
            m = jnp.where(strict, _mm3_nt(kb, kc) * dec, 0.0)
            tinv = _tri_inverse(m, eye, masks)
            eg = jnp.exp(gcc)
            sol = _mm3(tinv, jnp.concatenate([vb, kb * eg], axis=1))
            u_val, w_val = sol[:, :hd], sol[:, hd:]
            qk = jnp.where(lower, _mm_nt(qc.astype(BF16), kc.astype(BF16)) * dec, 0.0)
            qd = qc * eg
            glast = gcc[cs - 1:cs]
            kd = kc * jnp.exp(glast - gcc)
            state = s_ref[h]
            sb = state.astype(BF16)
            v_new = u_val - _mm(w_val.astype(BF16), sb)
            vnb = v_new.astype(BF16)
            o = _mm(qd.astype(BF16), sb) + _mm(qk.astype(BF16), vnb)
            s_ref[h] = state * jnp.exp(glast) + _mm_tn(kd.astype(BF16), vnb)
            o = o * lax.rsqrt(jnp.mean(o * o, axis=-1, keepdims=True) + RMS_EPS) * ng
            o_ref[rs, hs] = o * _silu(zh[rs])


def _gdn_mix(proj, blk, conv_w, alog, dtb, ng, layer, bsz, seq, n_heads):
    hps = GDN_HEADS_PER_STEP
    wblk = hps * GDN_HEAD_DIM
    ngrp = n_heads // hps
    tb = min(GDN_TB, seq)
    nt = seq // tb

    def tok(base):
        return pl.BlockSpec((tb, wblk), lambda b, g, t: (b * nt + t, base + g))

    def cw(base):
        return pl.BlockSpec((None, CONV_WIDTH, wblk), lambda b, g, t: (layer, 0, base + g))

    def vec128():
        return pl.BlockSpec((None, 1, LANES), lambda b, g, t: (layer * ngrp + g, 0, 0))

    return pl.pallas_call(
        _gdn_kernel,
        grid=(bsz, ngrp, nt),
        in_specs=[
            tok(blk['q']), tok(blk['k']), tok(blk['v']), tok(blk['z']),
            pl.BlockSpec((tb, LANES), lambda b, g, t: (b * nt + t, blk['ab'] + g)),
            cw(0), cw(ngrp), cw(2 * ngrp),
            vec128(), vec128(),
            pl.BlockSpec((None, 1, GDN_HEAD_DIM), lambda b, g, t: (layer, 0, 0)),
        ],
        out_specs=pl.BlockSpec((tb, wblk), lambda b, g, t: (b * nt + t, g)),
        out_shape=jax.ShapeDtypeStruct((bsz * seq, n_heads * GDN_HEAD_DIM), F32),
        scratch_shapes=[
            pltpu.VMEM((hps, GDN_HEAD_DIM, GDN_HEAD_DIM), F32),
            pltpu.VMEM((SUBLANES, wblk), F32), pltpu.VMEM((SUBLANES, wblk), F32), pltpu.VMEM((SUBLANES, wblk), F32),
        ],
        compiler_params=_cparams(("parallel", "parallel", "arbitrary")),
        name="gdn",
    )(proj, proj, proj, proj, proj, conv_w, conv_w, conv_w, alog, dtb, ng)


def _out_kernel(alpha, x_ref, ys_ref, yg0_ref, yg1_ref, yl_ref, wglu_ref, bglu_ref, w0_ref, w1_ref, w2_ref, w3_ref,
                g_ref, b_ref, o_ref):
    ys = _gelu_tanh(ys_ref[...])
    ys = ys * _sigmoid(_mm(ys.astype(BF16), wglu_ref[...]) + bglu_ref[...])
    mixed = _mm(ys.astype(BF16), w0_ref[...])
    mixed += _mm(yg0_ref[...].astype(BF16), w1_ref[...])
    mixed += _mm(yg1_ref[...].astype(BF16), w2_ref[...])
    mixed += _mm(yl_ref[...].astype(BF16), w3_ref[...])
    o_ref[...] = _layer_norm(alpha * x_ref[...] + mixed, g_ref[...], b_ref[...])


def _out_proj_ln(x, y_s5, y_gdn, y_lru, wglu, bglu, w_out, g, b, layer, alpha):
    t, d = x.shape
    wb = y_s5.shape[-1]
    tm = min(OUT_TM, t)

    def act(col):
        return pl.BlockSpec((tm, wb), lambda i: (i, col))

    def wrow(r):
        return pl.BlockSpec((None, wb, d), lambda i: (layer, r, 0))

    def vec(n):
        return pl.BlockSpec((None, 1, n), lambda i: (layer, 0, 0))

    return pl.pallas_call(
        functools.partial(_out_kernel, alpha),
        grid=(t // tm,),
        in_specs=[
            pl.BlockSpec((tm, d), lambda i: (i, 0)),
            act(0), act(0), act(1), act(0),
            pl.BlockSpec((None, wb, wb), lambda i: (layer, 0, 0)), vec(wb),
            wrow(0), wrow(1), wrow(2), wrow(3),
            vec(d), vec(d),
        ],
        out_specs=pl.BlockSpec((tm, d), lambda i: (i, 0)),
        out_shape=jax.ShapeDtypeStruct((t, d), F32),
        compiler_params=_cparams(("parallel",)),
        name="out_proj_ln",
    )(x, y_s5, y_gdn, y_gdn, y_lru, wglu, bglu, w_out, w_out, w_out, w_out, g, b)


def _row3(p):
    return p.astype(F32).reshape(p.shape[0], 1, p.shape[1])


def kernel(x, ffn1_w_gate, ffn1_w_up, ffn1_w_down, ln1_g, ln1_b, w_in, s5_lambda_re, s5_lambda_im, s5_b_re, s5_b_im,
           s5_c_re, s5_c_im, s5_d, s5_log_step, s5_w_glu, s5_b_glu, gdn_conv_w, gdn_a_log, gdn_dt_bias, gdn_norm_g,
           lru_conv_w, lru_conv_b, lru_w_a, lru_b_a, lru_w_x, lru_b_x, lru_lambda, w_out, ln2_g, ln2_b, ffn2_w_gate,
           ffn2_w_up, ffn2_w_down, ln3_g, ln3_b):
    bsz, seq, d_model = x.shape
    depth = w_in.shape[0]
    t = bsz * seq
    alpha = (2.0 * depth) ** 0.25
    s5_w = s5_d.shape[-1]
    n_heads = gdn_a_log.shape[-1]
    gdn_w = n_heads * GDN_HEAD_DIM
    lru_w = lru_lambda.shape[-1]
    lru_blocks, lru_blk = lru_w_a.shape[1], lru_w_a.shape[2]
    hps = GDN_HEADS_PER_STEP
    ngrp = n_heads // hps
    assert s5_w == PROJ_BLOCK and lru_w == PROJ_BLOCK and hps * GDN_HEAD_DIM == PROJ_BLOCK

    o_qkv = s5_w
    o_z = o_qkv + 3 * gdn_w
    o_a = o_z + gdn_w
    o_b = o_a + n_heads
    o_lx = o_b + n_heads
    o_lg = o_lx + lru_w
    main = jnp.concatenate([w_in[..., :o_a], w_in[..., o_lx:o_lg + lru_w]], axis=-1)
    ab_cols = []
    for g in range(ngrp):
        wa = w_in[..., o_a + g * hps:o_a + (g + 1) * hps]
        wb_ = w_in[..., o_b + g * hps:o_b + (g + 1) * hps]
        ab_cols += [wa, wb_, jnp.zeros(w_in.shape[:2] + (LANES - 2 * hps,), w_in.dtype)]
    n_main = main.shape[-1]
    n_proj = -(-(n_main + ngrp * LANES) // PROJ_TN) * PROJ_TN
    pad = jnp.zeros(w_in.shape[:2] + (n_proj - n_main - ngrp * LANES,), w_in.dtype)
    w_in_p = jnp.concatenate([main] + ab_cols + [pad], axis=-1).astype(BF16)
    blk = {'q': o_qkv // PROJ_BLOCK, 'k': (o_qkv + gdn_w) // PROJ_BLOCK, 'v': (o_qkv + 2 * gdn_w) // PROJ_BLOCK,
           'z': o_z // PROJ_BLOCK, 'ab': n_main // LANES}
    lx_blk = o_a // PROJ_BLOCK
    lg_blk = lx_blk + 1

    def ab_lanes(p):
        pg = p.astype(F32).reshape(depth, ngrp, hps)
        out = jnp.concatenate([pg, pg, jnp.zeros((depth, ngrp, LANES - 2 * hps), F32)], axis=-1)
        return out.reshape(depth * ngrp, 1, LANES)

    alog_l = ab_lanes(gdn_a_log)
    dtb_l = ab_lanes(gdn_dt_bias)

    eye_b = jnp.eye(lru_blocks, dtype=F32)

    def blockdiag(w):
        return jnp.einsum('lhij,hg->lhigj', w.astype(F32), eye_b).reshape(depth, lru_w, lru_w)

    lru_w2 = jnp.concatenate([blockdiag(lru_w_a), blockdiag(lru_w_x)], axis=-1)
    lru_wh, lru_wl = _split2(lru_w2)
    lru_bias = jnp.concatenate([lru_b_a, lru_b_x], axis=-1).astype(F32).reshape(depth, 1, 2 * lru_w)

    bf = lambda w: w.astype(BF16)
    f1g, f1u, f1d = bf(ffn1_w_gate), bf(ffn1_w_up), bf(ffn1_w_down)
    f2g, f2u, f2d = bf(ffn2_w_gate), bf(ffn2_w_up), bf(ffn2_w_down)
    w_out_b = bf(w_out)
    wglu_b = bf(s5_w_glu)

    h = x.reshape(t, d_model).astype(F32)
    for l in range(depth):
        h = _ffn_ln(h, f1g, f1u, f1d, _row3(ln1_g), _row3(ln1_b), l, alpha)
        proj = _in_proj(h, w_in_p, l)
        mats = _s5_matrices(s5_lambda_re[l], s5_lambda_im[l], s5_b_re[l], s5_b_im[l], s5_c_re[l], s5_c_im[l],
                            s5_d[l], s5_log_step[l], seq // S5_CHUNK)
        y_s5 = _s5_mix(proj[:, :s5_w], mats, bsz, seq)
        y_gdn = _gdn_mix(proj, blk, gdn_conv_w.astype(F32), alog_l, dtb_l, _row3(gdn_norm_g), l, bsz, seq, n_heads)
        y_lru = _lru_mix(proj, lx_blk, lg_blk, lru_conv_w.astype(F32), _row3(lru_conv_b), lru_wh, lru_wl, lru_bias,
                         _row3(lru_lambda), l, bsz, seq)
        h = _out_proj_ln(h, y_s5, y_gdn, y_lru, wglu_b, _row3(s5_b_glu), w_out_b, _row3(ln2_g), _row3(ln2_b), l,
                         alpha)
        h = _ffn_ln(h, f2g, f2u, f2d, _row3(ln3_g), _row3(ln3_b), l, alpha)
    return h.reshape(bsz, seq, d_model).astype(x.dtype)
```

```python
import functools
import math

import jax
import jax.numpy as jnp
from jax import lax
from jax.experimental import pallas as pl
from jax.experimental.pallas import tpu as pltpu

F32 = jnp.float32
BF16 = jnp.bfloat16

S5_GROUP = 16
S5_STATE = 64
S5_EIG_CLIP = -1e-4
S5_CHUNK = 16
GDN_HEAD_DIM = 128
GDN_CHUNK = 64
GDN_HEADS_PER_STEP = 8
GDN_SOLVE_PASSES = 1
LRU_C = 8.0
CONV_WIDTH = 4
LN_EPS = 1e-5
RMS_EPS = 1e-6
MACARON_WEIGHT = 0.5

LANES = 128
SUBLANES = 8
VMEM_LIMIT_BYTES = 56 * 1024 * 1024

FFN_TM = 1024
FFN_TF = 256
PROJ_TM = 1024
PROJ_TN = 1024
OUT_TM = 256
LRU_TL = 512
GDN_TB = 128
PROJ_BLOCK = 512


def _cparams(sem):
    return pltpu.CompilerParams(dimension_semantics=sem, vmem_limit_bytes=VMEM_LIMIT_BYTES)


def _split2(x):
    hi = x.astype(BF16)
    lo = (x - hi.astype(F32)).astype(BF16)
    return hi, lo


def _mm(a, b):
    return jnp.dot(a, b, preferred_element_type=F32)


def _mm_nt(a, b):
    return lax.dot_general(a, b, (((1,), (1,)), ((), ())), preferred_element_type=F32)


def _mm_tn(a, b):
    return lax.dot_general(a, b, (((0,), (0,)), ((), ())), preferred_element_type=F32)


def _mm3(a, b):
    ah, al = _split2(a)
    bh, bl = _split2(b)
    return _mm(ah, bh) + (_mm(al, bh) + _mm(ah, bl))


def _mm3_nt(a, b):
    ah, al = _split2(a)
    bh, bl = _split2(b)
    return _mm_nt(ah, bh) + (_mm_nt(al, bh) + _mm_nt(ah, bl))


def _mm3_pre(a, bh, bl):
    ah, al = _split2(a)
    return _mm(ah, bh) + (_mm(al, bh) + _mm(ah, bl))


def _layer_norm(y, g, b):
    mu = jnp.mean(y, axis=-1, keepdims=True)
    yc = y - mu
    var = jnp.mean(yc * yc, axis=-1, keepdims=True)
    return yc * lax.rsqrt(var + LN_EPS) * g + b


def _sigmoid(x):
    return 1.0 / (1.0 + jnp.exp(-x))


def _silu(x):
    return x * _sigmoid(x)


def _gelu_tanh(x):
    c = math.sqrt(2.0 / math.pi)
    return 0.5 * x * (1.0 + jnp.tanh(c * (x + 0.044715 * (x * x * x))))


def _softplus(x):
    return jnp.maximum(x, 0.0) + jnp.log1p(jnp.exp(-jnp.abs(x)))


def _shift_rows(x, s, row):
    return jnp.where(row >= s, pltpu.roll(x, s, 0), 0.0)


def _causal_conv(x, halo, cw):
    xp = jnp.concatenate([halo, x], axis=0)
    y = cw[CONV_WIDTH - 1:CONV_WIDTH] * x
    for s in range(1, CONV_WIDTH):
        k = CONV_WIDTH - 1 - s
        y = y + cw[k:k + 1] * pltpu.roll(xp, s, 0)[SUBLANES:]
    return y


def _ffn_kernel(alpha, x_ref, wg_ref, wu_ref, wd_ref, g_ref, b_ref, o_ref, xb_ref):
    j = pl.program_id(1)

    @pl.when(j == 0)
    def _():
        xb_ref[...] = x_ref[...].astype(BF16)
        o_ref[...] = jnp.zeros_like(o_ref)

    xb = xb_ref[...]
    gate = _mm(xb, wg_ref[...])
    up = _mm(xb, wu_ref[...])
    h = (_silu(gate) * up).astype(BF16)
    o_ref[...] += _mm(h, wd_ref[...])

    @pl.when(j == pl.num_programs(1) - 1)
    def _():
        y = alpha * x_ref[...] + MACARON_WEIGHT * o_ref[...]
        o_ref[...] = _layer_norm(y, g_ref[...], b_ref[...])


def _ffn_ln(x, wg, wu, wd, g, b, layer, alpha):
    t, d = x.shape
    f = wg.shape[-1]
    tm, tf = min(FFN_TM, t), min(FFN_TF, f)
    return pl.pallas_call(
        functools.partial(_ffn_kernel, alpha),
        grid=(t // tm, f // tf),
        in_specs=[
            pl.BlockSpec((tm, d), lambda i, j: (i, 0)),
            pl.BlockSpec((None, d, tf), lambda i, j: (layer, 0, j)),
            pl.BlockSpec((None, d, tf), lambda i, j: (layer, 0, j)),
            pl.BlockSpec((None, tf, d), lambda i, j: (layer, j, 0)),
            pl.BlockSpec((None, 1, d), lambda i, j: (layer, 0, 0)),
            pl.BlockSpec((None, 1, d), lambda i, j: (layer, 0, 0)),
        ],
        out_specs=pl.BlockSpec((tm, d), lambda i, j: (i, 0)),
        out_shape=jax.ShapeDtypeStruct((t, d), F32),
        scratch_shapes=[pltpu.VMEM((tm, d), BF16)],
        compiler_params=_cparams(("parallel", "arbitrary")),
        name="ffn_ln",
    )(x, wg, wu, wd, g, b)


def _proj_kernel(x_ref, w_ref, o_ref, xb_ref):
    @pl.when(pl.program_id(1) == 0)
    def _():
        xb_ref[...] = x_ref[...].astype(BF16)

    o_ref[...] = _mm(xb_ref[...], w_ref[...])


def _in_proj(x, w, layer):
    t, d = x.shape
    n = w.shape[-1]
    tm, tn = min(PROJ_TM, t), PROJ_TN
    return pl.pallas_call(
        _proj_kernel,
        grid=(t // tm, n // tn),
        in_specs=[
            pl.BlockSpec((tm, d), lambda i, j: (i, 0)),
            pl.BlockSpec((None, d, tn), lambda i, j: (layer, 0, j)),
        ],
        out_specs=pl.BlockSpec((tm, tn), lambda i, j: (i, j)),
        out_shape=jax.ShapeDtypeStruct((t, n), F32),
        scratch_shapes=[pltpu.VMEM((tm, d), BF16)],
        compiler_params=_cparams(("parallel", "arbitrary")),
        name="in_proj",
    )(x, w)


def _s5_kernel(seg, u_ref, th_ref, tl_ref, gh_ref, gl_ref, eh_ref, el_ref, p1_ref, p2_ref, d_ref, o_ref):
    u = u_ref[...]
    uh, ul = _split2(u)
    y = d_ref[...] * u + (_mm(uh, th_ref[...]) + (_mm(ul, th_ref[...]) + _mm(uh, tl_ref[...])))
    z = _mm(uh, gh_ref[...]) + (_mm(ul, gh_ref[...]) + _mm(uh, gl_ref[...]))
    row = lax.broadcasted_iota(jnp.int32, z.shape, 0) & (seg - 1)
    half = z.shape[1] // 2
    p1 = p1_ref[...]
    p2 = p2_ref[...]
    s, k = 1, 0
    while s < seg:
        zs = _shift_rows(z, s, row)
        z = z + (zs * p1[k:k + 1] + pltpu.roll(zs, half, 1) * p2[k:k + 1])
        s *= 2
        k += 1
    h_in = _shift_rows(z, 1, row)
    hh, hl = _split2(h_in)
    y = y + (_mm(hh, eh_ref[...]) + (_mm(hl, eh_ref[...]) + _mm(hh, el_ref[...])))
    o_ref[...] = y


def _s5_core(u_g, mats, seg):
    groups, rows, width = u_g.shape
    th, tl, gh, gl, eh, el, p1, p2, dflat = mats
    n2 = gh.shape[-1]

    def spec(a, b):
        return pl.BlockSpec((None, a, b), lambda g: (g, 0, 0))

    return pl.pallas_call(
        functools.partial(_s5_kernel, seg),
        grid=(groups,),
        in_specs=[spec(rows, width), spec(width, width), spec(width, width), spec(width, n2), spec(width, n2),
                  spec(n2, width), spec(n2, width), spec(p1.shape[1], n2), spec(p2.shape[1], n2), spec(1, width)],
        out_specs=spec(rows, width),
        out_shape=jax.ShapeDtypeStruct((groups, rows, width), F32),
        compiler_params=_cparams(("parallel",)),
        name="s5_core",
    )(u_g, th, tl, gh, gl, eh, el, p1, p2, dflat)


def _s5_matrices(lam_re, lam_im, b_re, b_im, c_re, c_im, d, log_step, seg):
    groups, n = lam_re.shape
    p = S5_GROUP
    cs = S5_CHUNK
    lam = lax.complex(jnp.minimum(lam_re.astype(F32), S5_EIG_CLIP), lam_im.astype(F32))
    dt = jnp.exp(log_step.astype(F32))[:, None]
    zdt = lam * dt
    lam_bar = jnp.exp(zdt)
    b_bar = ((lam_bar - 1.0) / lam)[..., None] * lax.complex(b_re.astype(F32), b_im.astype(F32))
    c_mat = lax.complex(c_re.astype(F32), c_im.astype(F32))
    taus = jnp.arange(cs + 1, dtype=F32)
    pw = jnp.exp(taus[None, :, None] * zdt[:, None, :])
    kmat = jnp.einsum('gpn,gtn,gnq->gtpq', c_mat, pw[:, :cs], b_bar).real
    ii = jnp.arange(cs)[:, None]
    jj = jnp.arange(cs)[None, :]
    lag = jnp.clip(jj - ii, 0, cs - 1)
    tmat = jnp.where((jj >= ii)[None, :, :, None, None], kmat[:, lag], 0.0)
    tmat = tmat.transpose(0, 1, 4, 2, 3).reshape(groups, cs * p, cs * p)
    gin = pw[:, cs - 1 - jnp.arange(cs)][:, :, :, None] * b_bar[:, None]
    gin = gin.transpose(0, 1, 3, 2).reshape(groups, cs * p, n)
    gmat = jnp.concatenate([gin.real, gin.imag], axis=-1)
    eo = c_mat[:, None] * pw[:, 1:cs + 1][:, :, None, :]
    eo = eo.transpose(0, 3, 1, 2).reshape(groups, n, cs * p)
    emat = jnp.concatenate([eo.real, -eo.imag], axis=1)
    nsteps = max(1, int(math.log2(seg)))
    ks = (cs * (2.0 ** jnp.arange(nsteps, dtype=F32)))
    mult = jnp.exp(ks[None, :, None] * zdt[:, None, :])
    p1 = jnp.concatenate([mult.real, mult.real], axis=-1)
    p2 = jnp.concatenate([-mult.imag, mult.imag], axis=-1)
    pad = (-nsteps) % SUBLANES
    p1 = jnp.pad(p1, ((0, 0), (0, pad), (0, 0)))
    p2 = jnp.pad(p2, ((0, 0), (0, pad), (0, 0)))
    dflat = jnp.tile(d.astype(F32).reshape(groups, 1, p), (1, 1, cs))
    th, tl = _split2(tmat)
    gh, gl = _split2(gmat)
    eh, el = _split2(emat)
    return th, tl, gh, gl, eh, el, p1, p2, dflat


def _s5_mix(u, mats, bsz, seq):
    t, width = u.shape
    groups = width // S5_GROUP
    nchunk = seq // S5_CHUNK
    u_g = u.reshape(bsz, nchunk, S5_CHUNK, groups, S5_GROUP).transpose(3, 0, 1, 2, 4)
    u_g = u_g.reshape(groups, bsz * nchunk, S5_CHUNK * S5_GROUP)
    y_g = _s5_core(u_g, mats, nchunk)
    y = y_g.reshape(groups, bsz, nchunk, S5_CHUNK, S5_GROUP).transpose(1, 2, 3, 0, 4)
    return y.reshape(t, width)


def _lru_kernel(x_ref, gate_ref, cw_ref, cb_ref, wh_ref, wl_ref, bias_ref, lam_ref, o_ref, halo_ref, h_ref):
    @pl.when(pl.program_id(1) == 0)
    def _():
        halo_ref[...] = jnp.zeros_like(halo_ref)
        h_ref[...] = jnp.zeros_like(h_ref)

    x = x_ref[...]
    rows, width = x.shape
    xc = _causal_conv(x, halo_ref[...], cw_ref[...]) + cb_ref[...]
    halo_ref[...] = x[rows - SUBLANES:rows]
    gates = _mm3_pre(xc, wh_ref[...], wl_ref[...]) + bias_ref[...]
    r = _sigmoid(gates[:, :width])
    i = _sigmoid(gates[:, width:])
    log_a = (-LRU_C) * r * _softplus(-lam_ref[...])
    a = jnp.exp(log_a)
    b = jnp.sqrt(-jnp.tanh(log_a) * (a * a + 1.0)) * (i * xc)
    row = lax.broadcasted_iota(jnp.int32, a.shape, 0)
    s = 1
    while s < rows:
        m = row >= s
        a_s = jnp.where(m, pltpu.roll(a, s, 0), 1.0)
        b_s = jnp.where(m, pltpu.roll(b, s, 0), 0.0)
        b = a * b_s + b
        a = a * a_s
        s *= 2
    h = a * h_ref[0:1] + b
    h_ref[0:1] = h[rows - 1:rows]
    o_ref[...] = h * _gelu_tanh(gate_ref[...])


def _lru_mix(proj, x_blk, gate_blk, cw, cb, wh, wl, bias, lam, layer, bsz, seq):
    width = cw.shape[-1]
    tl = min(LRU_TL, seq)
    nt = seq // tl

    def vec(rows):
        return pl.BlockSpec((None, rows, width), lambda b, c: (layer, 0, 0))

    return pl.pallas_call(
        _lru_kernel,
        grid=(bsz, nt),
        in_specs=[
            pl.BlockSpec((tl, width), lambda b, c: (b * nt + c, x_blk)),
            pl.BlockSpec((tl, width), lambda b, c: (b * nt + c, gate_blk)),
            vec(CONV_WIDTH), vec(1),
            pl.BlockSpec((None, width, 2 * width), lambda b, c: (layer, 0, 0)),
            pl.BlockSpec((None, width, 2 * width), lambda b, c: (layer, 0, 0)),
            pl.BlockSpec((None, 1, 2 * width), lambda b, c: (layer, 0, 0)),
            vec(1),
        ],
        out_specs=pl.BlockSpec((tl, width), lambda b, c: (b * nt + c, 0)),
        out_shape=jax.ShapeDtypeStruct((bsz * seq, width), F32),
        scratch_shapes=[pltpu.VMEM((SUBLANES, width), F32), pltpu.VMEM((SUBLANES, width), F32)],
        compiler_params=_cparams(("parallel", "arbitrary")),
        name="rglru",
    )(proj, proj, cw, cb, wh, wl, bias, lam)


def _parts(x, passes):
    if passes == 1:
        return (x.astype(BF16),)
    return _split2(x)


def _mmp(a, b, mm=_mm):
    out = mm(a[0], b[0])
    if len(a) > 1 and len(b) > 1:
        out = out + (mm(a[1], b[0]) + mm(a[0], b[1]))
    return out


def _tri_inverse_all(ms, eye, masks, passes):
    xs = [eye - jnp.where(masks[0], m, 0.0) for m in ms]
    for mk in masks[1:]:
        xp = [_parts(x, passes) for x in xs]
        mp = [_parts(jnp.where(mk, m, 0.0), passes) for m in ms]
        ys = [_mmp(a, b) for a, b in zip(xp, mp)]
        yp = [_parts(y, passes) for y in ys]
        xs = [x - _mmp(a, b) for x, a, b in zip(xs, yp, xp)]
    return xs


def _gdn_kernel(q_ref, k_ref, v_ref, z_ref, ab_ref, cq_ref, ck_ref, cv_ref, alog_ref, dtb_ref, ng_ref, o_ref,
                s_ref, hq_ref, hk_ref, hv_ref):
    @pl.when(pl.program_id(2) == 0)
    def _():
        s_ref[...] = jnp.zeros_like(s_ref)
        hq_ref[...] = jnp.zeros_like(hq_ref)
        hk_ref[...] = jnp.zeros_like(hk_ref)
        hv_ref[...] = jnp.zeros_like(hv_ref)

    rows = q_ref.shape[0]
    cs = GDN_CHUNK
    hd = GDN_HEAD_DIM
    nheads = GDN_HEADS_PER_STEP

    def conv_silu(x_ref, halo_ref, cw_ref):
        x = x_ref[...]
        y = _causal_conv(x, halo_ref[...], cw_ref[...])
        halo_ref[...] = x[rows - SUBLANES:rows]
        return _silu(y)

    q = conv_silu(q_ref, hq_ref, cq_ref)
    k = conv_silu(k_ref, hk_ref, ck_ref)
    v = conv_silu(v_ref, hv_ref, cv_ref)
    z = z_ref[...]

    ab = ab_ref[...]
    g_all = -jnp.exp(alog_ref[...]) * _softplus(ab + dtb_ref[...])
    beta_all = _sigmoid(ab)
    crow = lax.broadcasted_iota(jnp.int32, g_all.shape, 0) & (cs - 1)
    gc = g_all
    s = 1
    while s < cs:
        gc = gc + _shift_rows(gc, s, crow)
        s *= 2
    gc_t = gc.T

    ii = lax.broadcasted_iota(jnp.int32, (cs, cs), 0)
    jj = lax.broadcasted_iota(jnp.int32, (cs, cs), 1)
    lower = ii >= jj
    strict = ii > jj
    eye = jnp.where(ii == jj, 1.0, 0.0).astype(F32)
    masks = [strict & ((ii >> 1) == (jj >> 1))]
    lvl = 1
    while (1 << lvl) < cs:
        masks.append(strict & ((ii >> (lvl + 1)) == (jj >> (lvl + 1))) & ((ii >> lvl) != (jj >> lvl)))
        lvl += 1

    ng = ng_ref[...]
    nchunks = rows // cs
    heads = range(nheads)
    qn, kn = [], []
    for h in heads:
        hs = slice(h * hd, (h + 1) * hd)
        qh, kh = q[:, hs], k[:, hs]
        qn.append(qh * (lax.rsqrt(jnp.sum(qh * qh, axis=-1, keepdims=True) + RMS_EPS) * (hd ** -0.5)))
        kn.append(kh * lax.rsqrt(jnp.sum(kh * kh, axis=-1, keepdims=True) + RMS_EPS))

    probs = [(c, h) for c in range(nchunks) for h in heads]
    rs = lambda c: slice(c * cs, (c + 1) * cs)
    hsl = lambda h: slice(h * hd, (h + 1) * hd)
    qc = [qn[h][rs(c)] for c, h in probs]
    kc = [kn[h][rs(c)] for c, h in probs]
    bc = [beta_all[rs(c), nheads + h:nheads + h + 1] for c, h in probs]
    gcc = [gc[rs(c), h:h + 1] for c, h in probs]
    dec = [jnp.exp(jnp.where(lower, gcc[p] - gc_t[h:h + 1, rs(c)], -jnp.inf)) for p, (c, h) in enumerate(probs)]
    kb = [a * b for a, b in zip(kc, bc)]
    vb = [v[rs(c), hsl(h)] * bc[p] for p, (c, h) in enumerate(probs)]
    eg = [jnp.exp(g) for g in gcc]

    kcp = [_parts(a, GDN_SOLVE_PASSES) for a in kc]
    kbp = [_parts(a, GDN_SOLVE_PASSES) for a in kb]
    ms = [jnp.where(strict, _mmp(a, b, _mm_nt) * d, 0.0) for a, b, d in zip(kbp, kcp, dec)]
    tinv = _tri_inverse_all(ms, eye, masks, GDN_SOLVE_PASSES)
    rhs = [jnp.concatenate([a, b * e], axis=1) for a, b, e in zip(vb, kb, eg)]
    sol = [_mmp(_parts(t, GDN_SOLVE_PASSES), _parts(r, GDN_SOLVE_PASSES)) for t, r in zip(tinv, rhs)]
    qcb = [a.astype(BF16) for a in qc]
    qk = [jnp.where(lower, _mm_nt(a, b[0]) * d, 0.0).astype(BF16) for a, b, d in zip(qcb, kcp, dec)]
    qd = [(a * e).astype(BF16) for a, e in zip(qc, eg)]
    glast = [g[cs - 1:cs] for g in gcc]
    kd = [(a * jnp.exp(gl - g)).astype(BF16) for a, gl, g in zip(kc, glast, gcc)]

    state = [s_ref[h] for h in heads]
    for c in range(nchunks):
        ps = [c * nheads + h for h in heads]
        sb = [s.astype(BF16) for s in state]
        v_new = [sol[p][:, :hd] - _mm(sol[p][:, hd:].astype(BF16), sb[h]) for h, p in zip(heads, ps)]
        vnb = [a.astype(BF16) for a in v_new]
        outs = [_mm(qd[p], sb[h]) + _mm(qk[p], vnb[h]) for h, p in zip(heads, ps)]
        state = [state[h] * jnp.exp(glast[p]) + _mm_tn(kd[p], vnb[h]) for h, p in zip(heads, ps)]
        for h, o in zip(heads, outs):
            o = o * lax.rsqrt(jnp.mean(o * o, axis=-1, keepdims=True) + RMS_EPS) * ng
            o_ref[rs(c), hsl(h)] = o * _silu(z[rs(c), hsl(h)])
    for h in heads:
        s_ref[h] = state[h]


def _gdn_mix(proj, blk, conv_w, alog, dtb, ng, layer, bsz, seq, n_heads):
    hps = GDN_HEADS_PER_STEP
    wblk = hps * GDN_HEAD_DIM
    ngrp = n_heads // hps
    tb = min(GDN_TB, seq)
    nt = seq // tb

    def tok(base):
        return pl.BlockSpec((tb, wblk), lambda b, g, t: (b * nt + t, base + g))

    def cw(base):
        return pl.BlockSpec((None, CONV_WIDTH, wblk), lambda b, g, t: (layer, 0, base + g))

    def vec128():
        return pl.BlockSpec((None, 1, LANES), lambda b, g, t: (layer * ngrp + g, 0, 0))

    return pl.pallas_call(
        _gdn_kernel,
        grid=(bsz, ngrp, nt),
        in_specs=[
            tok(blk['q']), tok(blk['k']), tok(blk['v']), tok(blk['z']),
            pl.BlockSpec((tb, LANES), lambda b, g, t: (b * nt + t, blk['ab'] + g)),
            cw(0), cw(ngrp), cw(2 * ngrp),
            vec128(), vec128(),
            pl.BlockSpec((None, 1, GDN_HEAD_DIM), lambda b, g, t: (layer, 0, 0)),
        ],
        out_specs=pl.BlockSpec((tb, wblk), lambda b, g, t: (b * nt + t, g)),
        out_shape=jax.ShapeDtypeStruct((bsz * seq, n_heads * GDN_HEAD_DIM), F32),
        scratch_shapes=[
            pltpu.VMEM((hps, GDN_HEAD_DIM, GDN_HEAD_DIM), F32),
            pltpu.VMEM((SUBLANES, wblk), F32), pltpu.VMEM((SUBLANES, wblk), F32), pltpu.VMEM((SUBLANES, wblk), F32),
        ],
        compiler_params=_cparams(("parallel", "parallel", "arbitrary")),
        name="gdn",
    )(proj, proj, proj, proj, proj, conv_w, conv_w, conv_w, alog, dtb, ng)


def _out_kernel(alpha, x_ref, ys_ref, yg0_ref, yg1_ref, yl_ref, wglu_ref, bglu_ref, w0_ref, w1_ref, w2_ref, w3_ref,
                g_ref, b_ref, o_ref):
    ys = _gelu_tanh(ys_ref[...])
    ys = ys * _sigmoid(_mm(ys.astype(BF16), wglu_ref[...]) + bglu_ref[...])
    mixed = _mm(ys.astype(BF16), w0_ref[...])
    mixed += _mm(yg0_ref[...].astype(BF16), w1_ref[...])
    mixed += _mm(yg1_ref[...].astype(BF16), w2_ref[...])
    mixed += _mm(yl_ref[...].astype(BF16), w3_ref[...])
    o_ref[...] = _layer_norm(alpha * x_ref[...] + mixed, g_ref[...], b_ref[...])


def _out_proj_ln(x, y_s5, y_gdn, y_lru, wglu, bglu, w_out, g, b, layer, alpha):
    t, d = x.shape
    wb = y_s5.shape[-1]
    tm = min(OUT_TM, t)

    def act(col):
        return pl.BlockSpec((tm, wb), lambda i: (i, col))

    def wrow(r):
        return pl.BlockSpec((None, wb, d), lambda i: (layer, r, 0))

    def vec(n):
        return pl.BlockSpec((None, 1, n), lambda i: (layer, 0, 0))

    return pl.pallas_call(
        functools.partial(_out_kernel, alpha),
        grid=(t // tm,),
        in_specs=[
            pl.BlockSpec((tm, d), lambda i: (i, 0)),
            act(0), act(0), act(1), act(0),
            pl.BlockSpec((None, wb, wb), lambda i: (layer, 0, 0)), vec(wb),
            wrow(0), wrow(1), wrow(2), wrow(3),
            vec(d), vec(d),
        ],
        out_specs=pl.BlockSpec((tm, d), lambda i: (i, 0)),
        out_shape=jax.ShapeDtypeStruct((t, d), F32),
        compiler_params=_cparams(("parallel",)),
        name="out_proj_ln",
    )(x, y_s5, y_gdn, y_gdn, y_lru, wglu, bglu, w_out, w_out, w_out, w_out, g, b)


def _row3(p):
    return p.astype(F32).reshape(p.shape[0], 1, p.shape[1])


def kernel(x, ffn1_w_gate, ffn1_w_up, ffn1_w_down, ln1_g, ln1_b, w_in, s5_lambda_re, s5_lambda_im, s5_b_re, s5_b_im,
           s5_c_re, s5_c_im, s5_d, s5_log_step, s5_w_glu, s5_b_glu, gdn_conv_w, gdn_a_log, gdn_dt_bias, gdn_norm_g,
           lru_conv_w, lru_conv_b, lru_w_a, lru_b_a, lru_w_x, lru_b_x, lru_lambda, w_out, ln2_g, ln2_b, ffn2_w_gate,
           ffn2_w_up, ffn2_w_down, ln3_g, ln3_b):
    bsz, seq, d_model = x.shape
    depth = w_in.shape[0]
    t = bsz * seq
    alpha = (2.0 * depth) ** 0.25
    s5_w = s5_d.shape[-1]
    n_heads = gdn_a_log.shape[-1]
    gdn_w = n_heads * GDN_HEAD_DIM
    lru_w = lru_lambda.shape[-1]
    lru_blocks, lru_blk = lru_w_a.shape[1], lru_w_a.shape[2]
    hps = GDN_HEADS_PER_STEP
    ngrp = n_heads // hps
    wblk = hps * GDN_HEAD_DIM
    assert s5_w == PROJ_BLOCK and lru_w == PROJ_BLOCK and gdn_w % wblk == 0 and (4 * gdn_w) % PROJ_BLOCK == 0

    o_qkv = s5_w
    o_a = o_qkv + 4 * gdn_w
    o_b = o_a + n_heads
    o_lx = o_b + n_heads
    main = jnp.concatenate([w_in[..., o_qkv:o_a], w_in[..., :s5_w], w_in[..., o_lx:o_lx + 2 * lru_w]], axis=-1)
    ab_cols = []
    for g in range(ngrp):
        wa = w_in[..., o_a + g * hps:o_a + (g + 1) * hps]
        wb_ = w_in[..., o_b + g * hps:o_b + (g + 1) * hps]
        ab_cols += [wa, wb_, jnp.zeros(w_in.shape[:2] + (LANES - 2 * hps,), w_in.dtype)]
    n_main = main.shape[-1]
    n_proj = -(-(n_main + ngrp * LANES) // PROJ_TN) * PROJ_TN
    pad = jnp.zeros(w_in.shape[:2] + (n_proj - n_main - ngrp * LANES,), w_in.dtype)
    w_in_p = jnp.concatenate([main] + ab_cols + [pad], axis=-1).astype(BF16)
    blk = {'q': 0, 'k': gdn_w // wblk, 'v': 2 * gdn_w // wblk, 'z': 3 * gdn_w // wblk, 'ab': n_main // LANES}
    s5_col = 4 * gdn_w
    lx_blk = (s5_col + s5_w) // PROJ_BLOCK
    lg_blk = lx_blk + 1

    def ab_lanes(p):
        pg = p.astype(F32).reshape(depth, ngrp, hps)
        out = jnp.concatenate([pg, pg, jnp.zeros((depth, ngrp, LANES - 2 * hps), F32)], axis=-1)
        return out.reshape(depth * ngrp, 1, LANES)

    alog_l = ab_lanes(gdn_a_log)
    dtb_l = ab_lanes(gdn_dt_bias)

    eye_b = jnp.eye(lru_blocks, dtype=F32)

    def blockdiag(w):
        return jnp.einsum('lhij,hg->lhigj', w.astype(F32), eye_b).reshape(depth, lru_w, lru_w)

    lru_w2 = jnp.concatenate([blockdiag(lru_w_a), blockdiag(lru_w_x)], axis=-1)
    lru_wh, lru_wl = _split2(lru_w2)
    lru_bias = jnp.concatenate([lru_b_a, lru_b_x], axis=-1).astype(F32).reshape(depth, 1, 2 * lru_w)

    bf = lambda w: w.astype(BF16)
    f1g, f1u, f1d = bf(ffn1_w_gate), bf(ffn1_w_up), bf(ffn1_w_down)
    f2g, f2u, f2d = bf(ffn2_w_gate), bf(ffn2_w_up), bf(ffn2_w_down)
    w_out_b = bf(w_out)
    wglu_b = bf(s5_w_glu)

    h = x.reshape(t, d_model).astype(F32)
    for l in range(depth):
        h = _ffn_ln(h, f1g, f1u, f1d, _row3(ln1_g), _row3(ln1_b), l, alpha)
        proj = _in_proj(h, w_in_p, l)
        mats = _s5_matrices(s5_lambda_re[l], s5_lambda_im[l], s5_b_re[l], s5_b_im[l], s5_c_re[l], s5_c_im[l],
                            s5_d[l], s5_log_step[l], seq // S5_CHUNK)
        y_s5 = _s5_mix(proj[:, s5_col:s5_col + s5_w], mats, bsz, seq)
        y_gdn = _gdn_mix(proj, blk, gdn_conv_w.astype(F32), alog_l, dtb_l, _row3(gdn_norm_g), l, bsz, seq, n_heads)
        y_lru = _lru_mix(proj, lx_blk, lg_blk, lru_conv_w.astype(F32), _row3(lru_conv_b), lru_wh, lru_wl, lru_bias,
                         _row3(lru_lambda), l, bsz, seq)
        h = _out_proj_ln(h, y_s5, y_gdn, y_lru, wglu_b, _row3(s5_b_glu), w_out_b, _row3(ln2_g), _row3(ln2_b), l,
                         alpha)
        h = _ffn_ln(h, f2g, f2u, f2d, _row3(ln3_g), _row3(ln3_b), l, alpha)
    return h.reshape(bsz, seq, d_model).astype(x.dtype)
```

```python
import functools
import math

import jax
import jax.numpy as jnp
from jax import lax
from jax.experimental import pallas as pl
from jax.experimental.pallas import tpu as pltpu

F32 = jnp.float32
BF16 = jnp.bfloat16

S5_GROUP = 16
S5_STATE = 64
S5_EIG_CLIP = -1e-4
S5_CHUNK = 16
S5_ROWS = 256
S5_PASSES = 1
GDN_HEAD_DIM = 128
GDN_CHUNK = 64
GDN_HEADS_PER_STEP = 8
GDN_SOLVE_PASSES = 1
LRU_C = 8.0
CONV_WIDTH = 4
LN_EPS = 1e-5
RMS_EPS = 1e-6
MACARON_WEIGHT = 0.5

LANES = 128
SUBLANES = 8
VMEM_LIMIT_BYTES = 56 * 1024 * 1024

FFN_TM = 1024
FFN_TF = 256
PROJ_TM = 1024
PROJ_TN = 768
OUT_TM = 256
LRU_TL = 512
GDN_TB = 128
PROJ_BLOCK = 512


def _cparams(sem):
    return pltpu.CompilerParams(dimension_semantics=sem, vmem_limit_bytes=VMEM_LIMIT_BYTES)


def _split2(x):
    hi = x.astype(BF16)
    lo = (x - hi.astype(F32)).astype(BF16)
    return hi, lo


def _mm(a, b):
    return jnp.dot(a, b, preferred_element_type=F32)


def _mm_nt(a, b):
    return lax.dot_general(a, b, (((1,), (1,)), ((), ())), preferred_element_type=F32)


def _mm_tn(a, b):
    return lax.dot_general(a, b, (((0,), (0,)), ((), ())), preferred_element_type=F32)


def _mm3(a, b):
    ah, al = _split2(a)
    bh, bl = _split2(b)
    return _mm(ah, bh) + (_mm(al, bh) + _mm(ah, bl))


def _mm3_nt(a, b):
    ah, al = _split2(a)
    bh, bl = _split2(b)
    return _mm_nt(ah, bh) + (_mm_nt(al, bh) + _mm_nt(ah, bl))


def _mm3_pre(a, bh, bl):
    ah, al = _split2(a)
    return _mm(ah, bh) + (_mm(al, bh) + _mm(ah, bl))


def _layer_norm(y, g, b):
    mu = jnp.mean(y, axis=-1, keepdims=True)
    yc = y - mu
    var = jnp.mean(yc * yc, axis=-1, keepdims=True)
    return yc * lax.rsqrt(var + LN_EPS) * g + b


def _sigmoid(x):
    return 1.0 / (1.0 + jnp.exp(-x))


def _silu(x):
    return x * _sigmoid(x)


def _gelu_tanh(x):
    c = math.sqrt(2.0 / math.pi)
    return 0.5 * x * (1.0 + jnp.tanh(c * (x + 0.044715 * (x * x * x))))


def _softplus(x):
    return jnp.maximum(x, 0.0) + jnp.log1p(jnp.exp(-jnp.abs(x)))


def _shift_rows(x, s, row):
    return jnp.where(row >= s, pltpu.roll(x, s, 0), 0.0)


def _causal_conv(x, halo, cw):
    xp = jnp.concatenate([halo, x], axis=0)
    y = cw[CONV_WIDTH - 1:CONV_WIDTH] * x
    for s in range(1, CONV_WIDTH):
        k = CONV_WIDTH - 1 - s
        y = y + cw[k:k + 1] * pltpu.roll(xp, s, 0)[SUBLANES:]
    return y


def _ffn_kernel(alpha, x_ref, wg_ref, wu_ref, wd_ref, g_ref, b_ref, o_ref, xb_ref):
    j = pl.program_id(1)

    @pl.when(j == 0)
    def _():
        xb_ref[...] = x_ref[...].astype(BF16)
        o_ref[...] = jnp.zeros_like(o_ref)

    xb = xb_ref[...]
    gate = _mm(xb, wg_ref[...])
    up = _mm(xb, wu_ref[...])
    h = (_silu(gate) * up).astype(BF16)
    o_ref[...] += _mm(h, wd_ref[...])

    @pl.when(j == pl.num_programs(1) - 1)
    def _():
        y = alpha * x_ref[...] + MACARON_WEIGHT * o_ref[...]
        o_ref[...] = _layer_norm(y, g_ref[...], b_ref[...])


def _ffn_ln(x, wg, wu, wd, g, b, layer, alpha):
    t, d = x.shape
    f = wg.shape[-1]
    tm, tf = min(FFN_TM, t), min(FFN_TF, f)
    return pl.pallas_call(
        functools.partial(_ffn_kernel, alpha),
        grid=(t // tm, f // tf),
        in_specs=[
            pl.BlockSpec((tm, d), lambda i, j: (i, 0)),
            pl.BlockSpec((None, d, tf), lambda i, j: (layer, 0, j)),
            pl.BlockSpec((None, d, tf), lambda i, j: (layer, 0, j)),
            pl.BlockSpec((None, tf, d), lambda i, j: (layer, j, 0)),
            pl.BlockSpec((None, 1, d), lambda i, j: (layer, 0, 0)),
            pl.BlockSpec((None, 1, d), lambda i, j: (layer, 0, 0)),
        ],
        out_specs=pl.BlockSpec((tm, d), lambda i, j: (i, 0)),
        out_shape=jax.ShapeDtypeStruct((t, d), F32),
        scratch_shapes=[pltpu.VMEM((tm, d), BF16)],
        compiler_params=_cparams(("parallel", "arbitrary")),
        name="ffn_ln",
    )(x, wg, wu, wd, g, b)


def _proj_kernel(x_ref, w_ref, wu_ref, o_ref, u_ref, xb_ref):
    @pl.when(pl.program_id(1) == 0)
    def _():
        xb = x_ref[...].astype(BF16)
        xb_ref[...] = xb
        u = _mm(xb, wu_ref[...])
        for o in range(u_ref.shape[0]):
            u_ref[o] = u[:, o * LANES:(o + 1) * LANES]

    o_ref[...] = _mm(xb_ref[...], w_ref[...])


def _in_proj(x, w, w_u, layer):
    t, d = x.shape
    n = w.shape[-1]
    nu = w_u.shape[-1]
    tm, tn = min(PROJ_TM, t), PROJ_TN
    return pl.pallas_call(
        _proj_kernel,
        grid=(t // tm, n // tn),
        in_specs=[
            pl.BlockSpec((tm, d), lambda i, j: (i, 0)),
            pl.BlockSpec((None, d, tn), lambda i, j: (layer, 0, j)),
            pl.BlockSpec((None, d, nu), lambda i, j: (layer, 0, 0)),
        ],
        out_specs=[
            pl.BlockSpec((tm, tn), lambda i, j: (i, j)),
            pl.BlockSpec((nu // LANES, tm, LANES), lambda i, j: (0, i, 0)),
        ],
        out_shape=[jax.ShapeDtypeStruct((t, n), F32), jax.ShapeDtypeStruct((nu // LANES, t, LANES), F32)],
        scratch_shapes=[pltpu.VMEM((tm, d), BF16)],
        compiler_params=_cparams(("parallel", "arbitrary")),
        name="in_proj",
    )(x, w, w_u)


def _s5_kernel(blocks_per_seq, u_ref, t_ref, g_ref, e_ref, p1_ref, p2_ref, d_ref, o_ref, c_ref):
    @pl.when(pl.program_id(1) % blocks_per_seq == 0)
    def _():
        c_ref[...] = jnp.zeros_like(c_ref)

    u = u_ref[...]
    up = _parts(u, S5_PASSES)
    wt, wg, we = t_ref[...], g_ref[...], e_ref[...]
    y = d_ref[...] * u
    z = None
    for a in up:
        y = y + _mm(a, wt)
        z = _mm(a, wg) if z is None else z + _mm(a, wg)
    rows = z.shape[0]
    half = z.shape[1] // 2
    p1 = p1_ref[...]
    p2 = p2_ref[...]
    row = lax.broadcasted_iota(jnp.int32, z.shape, 0)
    carry = c_ref[0:1]
    z = z + jnp.where(row == 0, carry * p1[0:1] + pltpu.roll(carry, half, 1) * p2[0:1], 0.0)
    s, k = 1, 0
    while s < rows:
        zs = _shift_rows(z, s, row)
        z = z + (zs * p1[k:k + 1] + pltpu.roll(zs, half, 1) * p2[k:k + 1])
        s *= 2
        k += 1
    h_in = jnp.where(row == 0, carry, pltpu.roll(z, 1, 0))
    c_ref[0:1] = z[rows - 1:rows]
    for a in _parts(h_in, S5_PASSES):
        y = y + _mm(a, we)
    o_ref[...] = y


def _s5_core(u8, mats, layer, seq):
    t8, g8, e8, p1, p2, d8 = mats
    noct, t, lanes = u8.shape
    width = S5_CHUNK * lanes
    rows = t // S5_CHUNK
    nstate = g8.shape[-1]
    rb = min(S5_ROWS, seq // S5_CHUNK)
    u_rows = u8.reshape(noct, rows, width)

    def spec(a, b):
        return pl.BlockSpec((None, None, a, b), lambda o, r: (layer, o, 0, 0))

    y = pl.pallas_call(
        functools.partial(_s5_kernel, (seq // S5_CHUNK) // rb),
        grid=(noct, rows // rb),
        in_specs=[pl.BlockSpec((None, rb, width), lambda o, r: (o, r, 0)),
                  spec(width, width), spec(width, nstate), spec(nstate, width),
                  spec(p1.shape[2], nstate), spec(p2.shape[2], nstate), spec(1, width)],
        out_specs=pl.BlockSpec((None, rb, width), lambda o, r: (o, r, 0)),
        out_shape=jax.ShapeDtypeStruct((noct, rows, width), F32),
        scratch_shapes=[pltpu.VMEM((SUBLANES, nstate), F32)],
        compiler_params=_cparams(("parallel", "arbitrary")),
        name="s5_core",
    )(u_rows, t8, g8, e8, p1, p2, d8)
    return y.reshape(noct, t, lanes)


def _s5_matrices(lam_re, lam_im, b_re, b_im, c_re, c_im, d, log_step, nsteps):
    groups, n = lam_re.shape
    p = S5_GROUP
    cs = S5_CHUNK
    gpo = LANES // p
    noct = groups // gpo
    hi = lax.Precision.HIGHEST
    lr = jnp.minimum(lam_re.astype(F32), S5_EIG_CLIP)
    li = lam_im.astype(F32)
    dt = jnp.exp(log_step.astype(F32))[:, None]
    zr, zi = lr * dt, li * dt

    def powers(ex):
        mag = jnp.exp(ex[None, :, None] * zr[:, None, :])
        ang = ex[None, :, None] * zi[:, None, :]
        return mag * jnp.cos(ang), mag * jnp.sin(ang)

    pwr, pwi = powers(jnp.arange(cs + 1, dtype=F32))
    ar, ai = pwr[:, 1] - 1.0, pwi[:, 1]
    den = lr * lr + li * li
    cr, ci = (ar * lr + ai * li) / den, (ai * lr - ar * li) / den
    br, bi = b_re.astype(F32), b_im.astype(F32)
    bbr = cr[..., None] * br - ci[..., None] * bi
    bbi = cr[..., None] * bi + ci[..., None] * br
    ccr, cci = c_re.astype(F32), c_im.astype(F32)
    qr = pwr[:, :cs, :, None] * bbr[:, None] - pwi[:, :cs, :, None] * bbi[:, None]
    qi = pwr[:, :cs, :, None] * bbi[:, None] + pwi[:, :cs, :, None] * bbr[:, None]
    kmat = (jnp.einsum('gpn,gtnq->gtpq', ccr, qr, precision=hi)
            - jnp.einsum('gpn,gtnq->gtpq', cci, qi, precision=hi))
    ii = jnp.arange(cs)[:, None]
    jj = jnp.arange(cs)[None, :]
    lag = jnp.clip(jj - ii, 0, cs - 1)
    tmat = jnp.where((jj >= ii)[None, :, :, None, None], kmat[:, lag], 0.0)
    same = jnp.eye(gpo, dtype=bool)[None, None, :, None, None, :, None]

    def expand(a):
        a = a.reshape((noct, gpo) + a.shape[1:]).transpose(0, 2, 1, 3, 4, 5)[:, :, :, :, :, None, :]
        out = jnp.where(same, a, 0.0)
        return out.reshape(noct, a.shape[1] * gpo * a.shape[3], a.shape[4] * gpo * a.shape[6])

    t8 = expand(tmat.transpose(0, 1, 4, 2, 3))
    rev = cs - 1 - jnp.arange(cs)
    gin = jnp.stack([qr[:, rev], qi[:, rev]], axis=-1)
    g8 = expand(gin.transpose(0, 1, 3, 4, 2))
    er = ccr[:, None] * pwr[:, 1:cs + 1, None, :] - cci[:, None] * pwi[:, 1:cs + 1, None, :]
    ei = ccr[:, None] * pwi[:, 1:cs + 1, None, :] + cci[:, None] * pwr[:, 1:cs + 1, None, :]
    eo = jnp.stack([er, -ei], axis=1)
    e8 = expand(eo.transpose(0, 1, 4, 2, 3))
    mr, mi = powers(cs * (2.0 ** jnp.arange(nsteps, dtype=F32)))
    mr = mr.reshape(noct, gpo, nsteps, n).transpose(0, 2, 1, 3).reshape(noct, nsteps, gpo * n)
    mi = mi.reshape(noct, gpo, nsteps, n).transpose(0, 2, 1, 3).reshape(noct, nsteps, gpo * n)
    pad = ((0, 0), (0, (-nsteps) % SUBLANES), (0, 0))
    p1 = jnp.pad(jnp.concatenate([mr, mr], axis=-1), pad)
    p2 = jnp.pad(jnp.concatenate([-mi, mi], axis=-1), pad)
    d8 = jnp.tile(d.astype(F32).reshape(noct, 1, LANES), (1, 1, cs))
    return t8.astype(BF16), g8.astype(BF16), e8.astype(BF16), p1, p2, d8


def _lru_kernel(x_ref, gate_ref, cw_ref, cb_ref, wh_ref, wl_ref, bias_ref, lam_ref, o_ref, halo_ref, h_ref):
    @pl.when(pl.program_id(1) == 0)
    def _():
        halo_ref[...] = jnp.zeros_like(halo_ref)
        h_ref[...] = jnp.zeros_like(h_ref)

    x = x_ref[...]
    rows, width = x.shape
    xc = _causal_conv(x, halo_ref[...], cw_ref[...]) + cb_ref[...]
    halo_ref[...] = x[rows - SUBLANES:rows]
    gates = _mm3_pre(xc, wh_ref[...], wl_ref[...]) + bias_ref[...]
    r = _sigmoid(gates[:, :width])
    i = _sigmoid(gates[:, width:])
    log_a = (-LRU_C) * r * _softplus(-lam_ref[...])
    a = jnp.exp(log_a)
    b = jnp.sqrt(-jnp.tanh(log_a) * (a * a + 1.0)) * (i * xc)
    row = lax.broadcasted_iota(jnp.int32, a.shape, 0)
    s = 1
    while s < rows:
        m = row >= s
        a_s = jnp.where(m, pltpu.roll(a, s, 0), 1.0)
        b_s = jnp.where(m, pltpu.roll(b, s, 0), 0.0)
        b = a * b_s + b
        a = a * a_s
        s *= 2
    h = a * h_ref[0:1] + b
    h_ref[0:1] = h[rows - 1:rows]
    o_ref[...] = h * _gelu_tanh(gate_ref[...])


def _lru_mix(proj, x_blk, gate_blk, cw, cb, wh, wl, bias, lam, layer, bsz, seq):
    width = cw.shape[-1]
    tl = min(LRU_TL, seq)
    nt = seq // tl

    def vec(rows):
        return pl.BlockSpec((None, rows, width), lambda b, c: (layer, 0, 0))

    return pl.pallas_call(
        _lru_kernel,
        grid=(bsz, nt),
        in_specs=[
            pl.BlockSpec((tl, width), lambda b, c: (b * nt + c, x_blk)),
            pl.BlockSpec((tl, width), lambda b, c: (b * nt + c, gate_blk)),
            vec(CONV_WIDTH), vec(1),
            pl.BlockSpec((None, width, 2 * width), lambda b, c: (layer, 0, 0)),
            pl.BlockSpec((None, width, 2 * width), lambda b, c: (layer, 0, 0)),
            pl.BlockSpec((None, 1, 2 * width), lambda b, c: (layer, 0, 0)),
            vec(1),
        ],
        out_specs=pl.BlockSpec((tl, width), lambda b, c: (b * nt + c, 0)),
        out_shape=jax.ShapeDtypeStruct((bsz * seq, width), F32),
        scratch_shapes=[pltpu.VMEM((SUBLANES, width), F32), pltpu.VMEM((SUBLANES, width), F32)],
        compiler_params=_cparams(("parallel", "arbitrary")),
        name="rglru",
    )(proj, proj, cw, cb, wh, wl, bias, lam)


def _parts(x, passes):
    if passes == 1:
        return (x.astype(BF16),)
    return _split2(x)


def _mmp(a, b, mm=_mm):
    out = mm(a[0], b[0])
    if len(a) > 1 and len(b) > 1:
        out = out + (mm(a[1], b[0]) + mm(a[0], b[1]))
    return out


def _tri_inverse_all(ms, eye, masks, passes):
    xs = [eye - jnp.where(masks[0], m, 0.0) for m in ms]
    for mk in masks[1:]:
        xp = [_parts(x, passes) for x in xs]
        mp = [_parts(jnp.where(mk, m, 0.0), passes) for m in ms]
        ys = [_mmp(a, b) for a, b in zip(xp, mp)]
        yp = [_parts(y, passes) for y in ys]
        xs = [x - _mmp(a, b) for x, a, b in zip(xs, yp, xp)]
    return xs


def _gdn_kernel(q_ref, k_ref, v_ref, z_ref, ab_ref, cq_ref, ck_ref, cv_ref, alog_ref, dtb_ref, ng_ref, o_ref,
                s_ref, hq_ref, hk_ref, hv_ref):
    @pl.when(pl.program_id(2) == 0)
    def _():
        s_ref[...] = jnp.zeros_like(s_ref)
        hq_ref[...] = jnp.zeros_like(hq_ref)
        hk_ref[...] = jnp.zeros_like(hk_ref)
        hv_ref[...] = jnp.zeros_like(hv_ref)

    rows = q_ref.shape[0]
    cs = GDN_CHUNK
    hd = GDN_HEAD_DIM
    nheads = GDN_HEADS_PER_STEP

    def conv_silu(x_ref, halo_ref, cw_ref):
        x = x_ref[...]
        y = _causal_conv(x, halo_ref[...], cw_ref[...])
        halo_ref[...] = x[rows - SUBLANES:rows]
        return _silu(y)

    q = conv_silu(q_ref, hq_ref, cq_ref)
    k = conv_silu(k_ref, hk_ref, ck_ref)
    v = conv_silu(v_ref, hv_ref, cv_ref)
    z = z_ref[...]

    ab = ab_ref[...]
    g_all = -jnp.exp(alog_ref[...]) * _softplus(ab + dtb_ref[...])
    beta_all = _sigmoid(ab)
    crow = lax.broadcasted_iota(jnp.int32, g_all.shape, 0) & (cs - 1)
    gc = g_all
    s = 1
    while s < cs:
        gc = gc + _shift_rows(gc, s, crow)
        s *= 2
    gc_t = gc.T

    ii = lax.broadcasted_iota(jnp.int32, (cs, cs), 0)
    jj = lax.broadcasted_iota(jnp.int32, (cs, cs), 1)
    lower = ii >= jj
    strict = ii > jj
    eye = jnp.where(ii == jj, 1.0, 0.0).astype(F32)
    masks = [strict & ((ii >> 1) == (jj >> 1))]
    lvl = 1
    while (1 << lvl) < cs:
        masks.append(strict & ((ii >> (lvl + 1)) == (jj >> (lvl + 1))) & ((ii >> lvl) != (jj >> lvl)))
        lvl += 1

    ng = ng_ref[...]
    nchunks = rows // cs
    heads = range(nheads)
    qn, kn = [], []
    for h in heads:
        hs = slice(h * hd, (h + 1) * hd)
        qh, kh = q[:, hs], k[:, hs]
        qn.append(qh * (lax.rsqrt(jnp.sum(qh * qh, axis=-1, keepdims=True) + RMS_EPS) * (hd ** -0.5)))
        kn.append(kh * lax.rsqrt(jnp.sum(kh * kh, axis=-1, keepdims=True) + RMS_EPS))

    probs = [(c, h) for c in range(nchunks) for h in heads]
    rs = lambda c: slice(c * cs, (c + 1) * cs)
    hsl = lambda h: slice(h * hd, (h + 1) * hd)
    qc = [qn[h][rs(c)] for c, h in probs]
    kc = [kn[h][rs(c)] for c, h in probs]
    bc = [beta_all[rs(c), nheads + h:nheads + h + 1] for c, h in probs]
    gcc = [gc[rs(c), h:h + 1] for c, h in probs]
    dec = [jnp.exp(jnp.where(lower, gcc[p] - gc_t[h:h + 1, rs(c)], -jnp.inf)) for p, (c, h) in enumerate(probs)]
    kb = [a * b for a, b in zip(kc, bc)]
    vb = [v[rs(c), hsl(h)] * bc[p] for p, (c, h) in enumerate(probs)]
    eg = [jnp.exp(g) for g in gcc]

    kcp = [_parts(a, GDN_SOLVE_PASSES) for a in kc]
    kbp = [_parts(a, GDN_SOLVE_PASSES) for a in kb]
    ms = [jnp.where(strict, _mmp(a, b, _mm_nt) * d, 0.0) for a, b, d in zip(kbp, kcp, dec)]
    tinv = _tri_inverse_all(ms, eye, masks, GDN_SOLVE_PASSES)
    rhs = [jnp.concatenate([a, b * e], axis=1) for a, b, e in zip(vb, kb, eg)]
    sol = [_mmp(_parts(t, GDN_SOLVE_PASSES), _parts(r, GDN_SOLVE_PASSES)) for t, r in zip(tinv, rhs)]
    qcb = [a.astype(BF16) for a in qc]
    qk = [jnp.where(lower, _mm_nt(a, b[0]) * d, 0.0).astype(BF16) for a, b, d in zip(qcb, kcp, dec)]
    qd = [(a * e).astype(BF16) for a, e in zip(qc, eg)]
    glast = [g[cs - 1:cs] for g in gcc]
    kd = [(a * jnp.exp(gl - g)).astype(BF16) for a, gl, g in zip(kc, glast, gcc)]

    state = [s_ref[h] for h in heads]
    for c in range(nchunks):
        ps = [c * nheads + h for h in heads]
        sb = [s.astype(BF16) for s in state]
        v_new = [sol[p][:, :hd] - _mm(sol[p][:, hd:].astype(BF16), sb[h]) for h, p in zip(heads, ps)]
        vnb = [a.astype(BF16) for a in v_new]
        outs = [_mm(qd[p], sb[h]) + _mm(qk[p], vnb[h]) for h, p in zip(heads, ps)]
        state = [state[h] * jnp.exp(glast[p]) + _mm_tn(kd[p], vnb[h]) for h, p in zip(heads, ps)]
        for h, o in zip(heads, outs):
            o = o * lax.rsqrt(jnp.mean(o * o, axis=-1, keepdims=True) + RMS_EPS) * ng
            o_ref[rs(c), hsl(h)] = o * _silu(z[rs(c), hsl(h)])
    for h in heads:
        s_ref[h] = state[h]


def _gdn_mix(proj, blk, conv_w, alog, dtb, ng, layer, bsz, seq, n_heads):
    hps = GDN_HEADS_PER_STEP
    wblk = hps * GDN_HEAD_DIM
    ngrp = n_heads // hps
    tb = min(GDN_TB, seq)
    nt = seq // tb

    def tok(base):
        return pl.BlockSpec((tb, wblk), lambda b, g, t: (b * nt + t, base + g))

    def cw(base):
        return pl.BlockSpec((None, CONV_WIDTH, wblk), lambda b, g, t: (layer, 0, base + g))

    def vec128():
        return pl.BlockSpec((None, 1, LANES), lambda b, g, t: (layer * ngrp + g, 0, 0))

    return pl.pallas_call(
        _gdn_kernel,
        grid=(bsz, ngrp, nt),
        in_specs=[
            tok(blk['q']), tok(blk['k']), tok(blk['v']), tok(blk['z']),
            pl.BlockSpec((tb, LANES), lambda b, g, t: (b * nt + t, blk['ab'] + g)),
            cw(0), cw(ngrp), cw(2 * ngrp),
            vec128(), vec128(),
            pl.BlockSpec((None, 1, GDN_HEAD_DIM), lambda b, g, t: (layer, 0, 0)),
        ],
        out_specs=pl.BlockSpec((tb, wblk), lambda b, g, t: (b * nt + t, g)),
        out_shape=jax.ShapeDtypeStruct((bsz * seq, n_heads * GDN_HEAD_DIM), F32),
        scratch_shapes=[
            pltpu.VMEM((hps, GDN_HEAD_DIM, GDN_HEAD_DIM), F32),
            pltpu.VMEM((SUBLANES, wblk), F32), pltpu.VMEM((SUBLANES, wblk), F32), pltpu.VMEM((SUBLANES, wblk), F32),
        ],
        compiler_params=_cparams(("parallel", "parallel", "arbitrary")),
        name="gdn",
    )(proj, proj, proj, proj, proj, conv_w, conv_w, conv_w, alog, dtb, ng)


def _out_kernel(alpha, x_ref, ys_ref, yg0_ref, yg1_ref, yl_ref, wglu_ref, bglu_ref, w0_ref, w1_ref, w2_ref, w3_ref,
                g_ref, b_ref, o_ref):
    ys = _gelu_tanh(jnp.concatenate([ys_ref[o] for o in range(ys_ref.shape[0])], axis=1))
    ys = ys * _sigmoid(_mm(ys.astype(BF16), wglu_ref[...]) + bglu_ref[...])
    mixed = _mm(ys.astype(BF16), w0_ref[...])
    mixed += _mm(yg0_ref[...].astype(BF16), w1_ref[...])
    mixed += _mm(yg1_ref[...].astype(BF16), w2_ref[...])
    mixed += _mm(yl_ref[...].astype(BF16), w3_ref[...])
    o_ref[...] = _layer_norm(alpha * x_ref[...] + mixed, g_ref[...], b_ref[...])


def _out_proj_ln(x, y_s5, y_gdn, y_lru, wglu, bglu, w_out, g, b, layer, alpha):
    t, d = x.shape
    noct = y_s5.shape[0]
    wb = noct * y_s5.shape[-1]
    tm = min(OUT_TM, t)

    def act(col):
        return pl.BlockSpec((tm, wb), lambda i: (i, col))

    def wrow(r):
        return pl.BlockSpec((None, wb, d), lambda i: (layer, r, 0))

    def vec(n):
        return pl.BlockSpec((None, 1, n), lambda i: (layer, 0, 0))

    return pl.pallas_call(
        functools.partial(_out_kernel, alpha),
        grid=(t // tm,),
        in_specs=[
            pl.BlockSpec((tm, d), lambda i: (i, 0)),
            pl.BlockSpec((noct, tm, y_s5.shape[-1]), lambda i: (0, i, 0)),
            act(0), act(1), act(0),
            pl.BlockSpec((None, wb, wb), lambda i: (layer, 0, 0)), vec(wb),
            wrow(0), wrow(1), wrow(2), wrow(3),
            vec(d), vec(d),
        ],
        out_specs=pl.BlockSpec((tm, d), lambda i: (i, 0)),
        out_shape=jax.ShapeDtypeStruct((t, d), F32),
        compiler_params=_cparams(("parallel",)),
        name="out_proj_ln",
    )(x, y_s5, y_gdn, y_gdn, y_lru, wglu, bglu, w_out, w_out, w_out, w_out, g, b)


def _row3(p):
    return p.astype(F32).reshape(p.shape[0], 1, p.shape[1])


def kernel(x, ffn1_w_gate, ffn1_w_up, ffn1_w_down, ln1_g, ln1_b, w_in, s5_lambda_re, s5_lambda_im, s5_b_re, s5_b_im,
           s5_c_re, s5_c_im, s5_d, s5_log_step, s5_w_glu, s5_b_glu, gdn_conv_w, gdn_a_log, gdn_dt_bias, gdn_norm_g,
           lru_conv_w, lru_conv_b, lru_w_a, lru_b_a, lru_w_x, lru_b_x, lru_lambda, w_out, ln2_g, ln2_b, ffn2_w_gate,
           ffn2_w_up, ffn2_w_down, ln3_g, ln3_b):
    bsz, seq, d_model = x.shape
    depth = w_in.shape[0]
    t = bsz * seq
    alpha = (2.0 * depth) ** 0.25
    s5_w = s5_d.shape[-1]
    n_heads = gdn_a_log.shape[-1]
    gdn_w = n_heads * GDN_HEAD_DIM
    lru_w = lru_lambda.shape[-1]
    lru_blocks, lru_blk = lru_w_a.shape[1], lru_w_a.shape[2]
    hps = GDN_HEADS_PER_STEP
    ngrp = n_heads // hps
    wblk = hps * GDN_HEAD_DIM
    assert s5_w == PROJ_BLOCK and lru_w == PROJ_BLOCK and gdn_w % wblk == 0 and (4 * gdn_w) % PROJ_BLOCK == 0

    o_qkv = s5_w
    o_a = o_qkv + 4 * gdn_w
    o_b = o_a + n_heads
    o_lx = o_b + n_heads
    main = jnp.concatenate([w_in[..., o_qkv:o_a], w_in[..., o_lx:o_lx + 2 * lru_w]], axis=-1)
    w_u = w_in[..., :s5_w].astype(BF16)
    ab_cols = []
    for g in range(ngrp):
        wa = w_in[..., o_a + g * hps:o_a + (g + 1) * hps]
        wb_ = w_in[..., o_b + g * hps:o_b + (g + 1) * hps]
        ab_cols += [wa, wb_, jnp.zeros(w_in.shape[:2] + (LANES - 2 * hps,), w_in.dtype)]
    n_main = main.shape[-1]
    n_proj = -(-(n_main + ngrp * LANES) // PROJ_TN) * PROJ_TN
    pad = jnp.zeros(w_in.shape[:2] + (n_proj - n_main - ngrp * LANES,), w_in.dtype)
    w_in_p = jnp.concatenate([main] + ab_cols + [pad], axis=-1).astype(BF16)
    blk = {'q': 0, 'k': gdn_w // wblk, 'v': 2 * gdn_w // wblk, 'z': 3 * gdn_w // wblk, 'ab': n_main // LANES}
    lx_blk = (4 * gdn_w) // PROJ_BLOCK
    lg_blk = lx_blk + 1

    def ab_lanes(p):
        pg = p.astype(F32).reshape(depth, ngrp, hps)
        out = jnp.concatenate([pg, pg, jnp.zeros((depth, ngrp, LANES - 2 * hps), F32)], axis=-1)
        return out.reshape(depth * ngrp, 1, LANES)

    alog_l = ab_lanes(gdn_a_log)
    dtb_l = ab_lanes(gdn_dt_bias)

    eye_b = jnp.eye(lru_blocks, dtype=F32)

    def blockdiag(w):
        return jnp.einsum('lhij,hg->lhigj', w.astype(F32), eye_b).reshape(depth, lru_w, lru_w)

    lru_w2 = jnp.concatenate([blockdiag(lru_w_a), blockdiag(lru_w_x)], axis=-1)
    lru_wh, lru_wl = _split2(lru_w2)
    lru_bias = jnp.concatenate([lru_b_a, lru_b_x], axis=-1).astype(F32).reshape(depth, 1, 2 * lru_w)

    bf = lambda w: w.astype(BF16)
    f1g, f1u, f1d = bf(ffn1_w_gate), bf(ffn1_w_up), bf(ffn1_w_down)
    f2g, f2u, f2d = bf(ffn2_w_gate), bf(ffn2_w_up), bf(ffn2_w_down)
    w_out_b = bf(w_out)
    wglu_b = bf(s5_w_glu)

    s5_steps = int(math.log2(min(S5_ROWS, seq // S5_CHUNK)))
    s5_mats = jax.vmap(functools.partial(_s5_matrices, nsteps=s5_steps))(
        s5_lambda_re, s5_lambda_im, s5_b_re, s5_b_im, s5_c_re, s5_c_im, s5_d, s5_log_step)

    h = x.reshape(t, d_model).astype(F32)
    for l in range(depth):
        h = _ffn_ln(h, f1g, f1u, f1d, _row3(ln1_g), _row3(ln1_b), l, alpha)
        proj, u8 = _in_proj(h, w_in_p, w_u, l)
        y_s5 = _s5_core(u8, s5_mats, l, seq)
        y_gdn = _gdn_mix(proj, blk, gdn_conv_w.astype(F32), alog_l, dtb_l, _row3(gdn_norm_g), l, bsz, seq, n_heads)
        y_lru = _lru_mix(proj, lx_blk, lg_blk, lru_conv_w.astype(F32), _row3(lru_conv_b), lru_wh, lru_wl, lru_bias,
                         _row3(lru_lambda), l, bsz, seq)
        h = _out_proj_ln(h, y_s5, y_gdn, y_lru, wglu_b, _row3(s5_b_glu), w_out_b, _row3(ln2_g), _row3(ln2_b), l,
                         alpha)
        h = _ffn_ln(h, f2g, f2u, f2d, _row3(ln3_g), _row3(ln3_b), l, alpha)
    return h.reshape(bsz, seq, d_model).astype(x.dtype)
```

```python
import functools
import math

import jax
import jax.numpy as jnp
from jax import lax
from jax.experimental import pallas as pl
from jax.experimental.pallas import tpu as pltpu

F32 = jnp.float32
BF16 = jnp.bfloat16

S5_GROUP = 16
S5_STATE = 64
S5_EIG_CLIP = -1e-4
S5_CHUNK = 16
S5_ROWS = 256
S5_PASSES = 1
GDN_HEAD_DIM = 128
GDN_CHUNK = 64
GDN_HEADS_PER_STEP = 8
GDN_SOLVE_PASSES = 1
LRU_C = 8.0
CONV_WIDTH = 4
LN_EPS = 1e-5
RMS_EPS = 1e-6
MACARON_WEIGHT = 0.5

LANES = 128
SUBLANES = 8
VMEM_LIMIT_BYTES = 56 * 1024 * 1024

FFN_TM = 1024
FFN_TF = 256
PROJ_TM = 1024
PROJ_TN = 768
OUT_TM = 256
LRU_TL = 512
GDN_TB = 128
PROJ_BLOCK = 512


def _cparams(sem):
    return pltpu.CompilerParams(dimension_semantics=sem, vmem_limit_bytes=VMEM_LIMIT_BYTES)


def _split2(x):
    hi = x.astype(BF16)
    lo = (x - hi.astype(F32)).astype(BF16)
    return hi, lo


def _mm(a, b):
    return jnp.dot(a, b, preferred_element_type=F32)


def _mm_nt(a, b):
    return lax.dot_general(a, b, (((1,), (1,)), ((), ())), preferred_element_type=F32)


def _mm_tn(a, b):
    return lax.dot_general(a, b, (((0,), (0,)), ((), ())), preferred_element_type=F32)


def _mm3(a, b):
    ah, al = _split2(a)
    bh, bl = _split2(b)
    return _mm(ah, bh) + (_mm(al, bh) + _mm(ah, bl))


def _mm3_nt(a, b):
    ah, al = _split2(a)
    bh, bl = _split2(b)
    return _mm_nt(ah, bh) + (_mm_nt(al, bh) + _mm_nt(ah, bl))


def _mm3_pre(a, bh, bl):
    ah, al = _split2(a)
    return _mm(ah, bh) + (_mm(al, bh) + _mm(ah, bl))


def _layer_norm(y, g, b):
    mu = jnp.mean(y, axis=-1, keepdims=True)
    yc = y - mu
    var = jnp.mean(yc * yc, axis=-1, keepdims=True)
    return yc * lax.rsqrt(var + LN_EPS) * g + b


def _sigmoid(x):
    return 1.0 / (1.0 + jnp.exp(-x))


def _silu(x):
    return x * _sigmoid(x)


def _gelu_tanh(x):
    c = math.sqrt(2.0 / math.pi)
    return 0.5 * x * (1.0 + jnp.tanh(c * (x + 0.044715 * (x * x * x))))


def _softplus(x):
    return jnp.maximum(x, 0.0) + jnp.log1p(jnp.exp(-jnp.abs(x)))


def _shift_rows(x, s, row):
    return jnp.where(row >= s, pltpu.roll(x, s, 0), 0.0)


def _causal_conv(x, halo, cw):
    xp = jnp.concatenate([halo, x], axis=0)
    y = cw[CONV_WIDTH - 1:CONV_WIDTH] * x
    for s in range(1, CONV_WIDTH):
        k = CONV_WIDTH - 1 - s
        y = y + cw[k:k + 1] * pltpu.roll(xp, s, 0)[SUBLANES:]
    return y


def _ffn_kernel(alpha, x_ref, wg_ref, wu_ref, wd_ref, g_ref, b_ref, o_ref, xb_ref):
    j = pl.program_id(1)

    @pl.when(j == 0)
    def _():
        x = x_ref[...]
        xb_ref[...] = x.astype(BF16)
        o_ref[...] = (alpha / MACARON_WEIGHT) * x

    xb = xb_ref[...]
    gate = _mm(xb, wg_ref[...])
    up = _mm(xb, wu_ref[...])
    h = (_silu(gate) * up).astype(BF16)
    o_ref[...] += _mm(h, wd_ref[...])

    @pl.when(j == pl.num_programs(1) - 1)
    def _():
        o_ref[...] = _layer_norm(MACARON_WEIGHT * o_ref[...], g_ref[...], b_ref[...])


def _ffn_ln(x, wg, wu, wd, g, b, layer, alpha):
    t, d = x.shape
    f = wg.shape[-1]
    tm, tf = min(FFN_TM, t), min(FFN_TF, f)
    return pl.pallas_call(
        functools.partial(_ffn_kernel, alpha),
        grid=(t // tm, f // tf),
        in_specs=[
            pl.BlockSpec((tm, d), lambda i, j: (i, 0)),
            pl.BlockSpec((None, d, tf), lambda i, j: (layer, 0, j)),
            pl.BlockSpec((None, d, tf), lambda i, j: (layer, 0, j)),
            pl.BlockSpec((None, tf, d), lambda i, j: (layer, j, 0)),
            pl.BlockSpec((None, 1, d), lambda i, j: (layer, 0, 0)),
            pl.BlockSpec((None, 1, d), lambda i, j: (layer, 0, 0)),
        ],
        out_specs=pl.BlockSpec((tm, d), lambda i, j: (i, 0)),
        out_shape=jax.ShapeDtypeStruct((t, d), F32),
        scratch_shapes=[pltpu.VMEM((tm, d), BF16)],
        compiler_params=_cparams(("parallel", "arbitrary")),
        name="ffn_ln",
    )(x, wg, wu, wd, g, b)


def _proj_kernel(x_ref, w_ref, wu_ref, o_ref, u_ref, xb_ref):
    @pl.when(pl.program_id(1) == 0)
    def _():
        xb = x_ref[...].astype(BF16)
        xb_ref[...] = xb
        u = _mm(xb, wu_ref[...])
        for o in range(u_ref.shape[0]):
            u_ref[o] = u[:, o * LANES:(o + 1) * LANES]

    o_ref[...] = _mm(xb_ref[...], w_ref[...])


def _in_proj(x, w, w_u, layer):
    t, d = x.shape
    n = w.shape[-1]
    nu = w_u.shape[-1]
    tm, tn = min(PROJ_TM, t), PROJ_TN
    return pl.pallas_call(
        _proj_kernel,
        grid=(t // tm, n // tn),
        in_specs=[
            pl.BlockSpec((tm, d), lambda i, j: (i, 0)),
            pl.BlockSpec((None, d, tn), lambda i, j: (layer, 0, j)),
            pl.BlockSpec((None, d, nu), lambda i, j: (layer, 0, 0)),
        ],
        out_specs=[
            pl.BlockSpec((tm, tn), lambda i, j: (i, j)),
            pl.BlockSpec((nu // LANES, tm, LANES), lambda i, j: (0, i, 0)),
        ],
        out_shape=[jax.ShapeDtypeStruct((t, n), F32), jax.ShapeDtypeStruct((nu // LANES, t, LANES), F32)],
        scratch_shapes=[pltpu.VMEM((tm, d), BF16)],
        compiler_params=_cparams(("parallel", "arbitrary")),
        name="in_proj",
    )(x, w, w_u)


def _s5_expand_operators(k_ref, q_ref, ec_ref, t_ref, g_ref, e_ref):
    cs = k_ref.shape[0]
    n = ec_ref.shape[1] // 2
    gpo = LANES // S5_GROUP
    rgrp = lax.broadcasted_iota(jnp.int32, (LANES, LANES), 0) // S5_GROUP
    lane = lax.broadcasted_iota(jnp.int32, (LANES, LANES), 1)
    lgrp = lane // S5_GROUP
    zero = jnp.zeros((LANES, LANES), BF16)
    bd = [jnp.where(rgrp == lgrp, k_ref[tau], 0.0).astype(BF16) for tau in range(cs)]
    for i in range(cs):
        for j in range(cs):
            t_ref[i * LANES:(i + 1) * LANES, j * LANES:(j + 1) * LANES] = bd[j - i] if j >= i else zero
    lhalf = lane // n
    per_tile = LANES // n
    for i in range(cs):
        q2 = q_ref[i]
        for tile in range(2 * gpo // per_tile):
            c, k = divmod(tile, gpo // per_tile)
            x = q2[:, c * LANES:(c + 1) * LANES]
            g_ref[i * LANES:(i + 1) * LANES, tile * LANES:(tile + 1) * LANES] = jnp.where(
                rgrp == per_tile * k + lhalf, x, 0.0).astype(BF16)
    lgrp_n = lax.broadcasted_iota(jnp.int32, (n, LANES), 1) // S5_GROUP
    for j in range(cs):
        e = ec_ref[j]
        for c in range(2):
            ec = e[c * n:(c + 1) * n]
            for h in range(gpo):
                r0 = (c * gpo + h) * n
                e_ref[r0:r0 + n, j * LANES:(j + 1) * LANES] = jnp.where(lgrp_n == h, ec, 0.0).astype(BF16)


def _s5_kernel(blocks_per_seq, u_ref, k_ref, q_ref, ec_ref, p1_ref, p2_ref, d_ref, o_ref, c_ref, t_ref, g_ref,
               e_ref):
    @pl.when(pl.program_id(1) == 0)
    def _():
        _s5_expand_operators(k_ref, q_ref, ec_ref, t_ref, g_ref, e_ref)

    @pl.when(pl.program_id(1) % blocks_per_seq == 0)
    def _():
        c_ref[...] = jnp.zeros_like(c_ref)

    u = u_ref[...]
    up = _parts(u, S5_PASSES)
    wt, wg, we = t_ref[...], g_ref[...], e_ref[...]
    y = d_ref[...] * u
    z = None
    for a in up:
        y = y + _mm(a, wt)
        z = _mm(a, wg) if z is None else z + _mm(a, wg)
    rows = z.shape[0]
    half = z.shape[1] // 2
    p1 = p1_ref[...]
    p2 = p2_ref[...]
    row = lax.broadcasted_iota(jnp.int32, z.shape, 0)
    carry = c_ref[0:1]
    z = z + jnp.where(row == 0, carry * p1[0:1] + pltpu.roll(carry, half, 1) * p2[0:1], 0.0)
    s, k = 1, 0
    while s < rows:
        zs = _shift_rows(z, s, row)
        z = z + (zs * p1[k:k + 1] + pltpu.roll(zs, half, 1) * p2[k:k + 1])
        s *= 2
        k += 1
    h_in = jnp.where(row == 0, carry, pltpu.roll(z, 1, 0))
    c_ref[0:1] = z[rows - 1:rows]
    for a in _parts(h_in, S5_PASSES):
        y = y + _mm(a, we)
    o_ref[...] = y


def _s5_core(u8, mats, layer, seq):
    kt, q2, ec, p1, p2, d8 = mats
    noct, t, lanes = u8.shape
    width = S5_CHUNK * lanes
    rows = t // S5_CHUNK
    nstate = p1.shape[-1]
    rb = min(S5_ROWS, seq // S5_CHUNK)
    u_rows = u8.reshape(noct, rows, width)

    def spec(a):
        return pl.BlockSpec((None, None) + a.shape[2:], lambda o, r: (layer, o) + (0,) * (a.ndim - 2))

    y = pl.pallas_call(
        functools.partial(_s5_kernel, (seq // S5_CHUNK) // rb),
        grid=(noct, rows // rb),
        in_specs=[pl.BlockSpec((None, rb, width), lambda o, r: (o, r, 0)),
                  spec(kt), spec(q2), spec(ec), spec(p1), spec(p2), spec(d8)],
        out_specs=pl.BlockSpec((None, rb, width), lambda o, r: (o, r, 0)),
        out_shape=jax.ShapeDtypeStruct((noct, rows, width), F32),
        scratch_shapes=[pltpu.VMEM((SUBLANES, nstate), F32), pltpu.VMEM((width, width), BF16),
                        pltpu.VMEM((width, nstate), BF16), pltpu.VMEM((nstate, width), BF16)],
        compiler_params=_cparams(("parallel", "arbitrary")),
        name="s5_core",
    )(u_rows, kt, q2, ec, p1, p2, d8)
    return y.reshape(noct, t, lanes)


def _s5_matrices(lam_re, lam_im, b_re, b_im, c_re, c_im, d, log_step, nsteps):
    groups, n = lam_re.shape
    p = S5_GROUP
    cs = S5_CHUNK
    gpo = LANES // p
    noct = groups // gpo
    hi = lax.Precision.HIGHEST
    lr = jnp.minimum(lam_re.astype(F32), S5_EIG_CLIP)
    li = lam_im.astype(F32)
    dt = jnp.exp(log_step.astype(F32))[:, None]
    zr, zi = lr * dt, li * dt

    def powers(ex):
        mag = jnp.exp(ex[None, :, None] * zr[:, None, :])
        ang = ex[None, :, None] * zi[:, None, :]
        return mag * jnp.cos(ang), mag * jnp.sin(ang)

    pwr, pwi = powers(jnp.arange(cs + 1, dtype=F32))
    ar, ai = pwr[:, 1] - 1.0, pwi[:, 1]
    den = lr * lr + li * li
    cr, ci = (ar * lr + ai * li) / den, (ai * lr - ar * li) / den
    br, bi = b_re.astype(F32), b_im.astype(F32)
    bbr = cr[..., None] * br - ci[..., None] * bi
    bbi = cr[..., None] * bi + ci[..., None] * br
    ccr, cci = c_re.astype(F32), c_im.astype(F32)
    qr = pwr[:, :cs, :, None] * bbr[:, None] - pwi[:, :cs, :, None] * bbi[:, None]
    qi = pwr[:, :cs, :, None] * bbi[:, None] + pwi[:, :cs, :, None] * bbr[:, None]
    kmat = (jnp.einsum('gpn,gtnq->gtpq', ccr, qr, precision=hi)
            - jnp.einsum('gpn,gtnq->gtpq', cci, qi, precision=hi))
    kt = kmat.reshape(noct, gpo, cs, p, p).transpose(0, 2, 1, 4, 3).reshape(noct, cs, gpo * p, p)
    kt = jnp.tile(kt, (1, 1, 1, gpo))
    rev = cs - 1 - jnp.arange(cs)

    def rows_gp(a):
        return a.reshape(noct, gpo, cs, n, p).transpose(0, 2, 1, 4, 3).reshape(noct, cs, gpo * p, n)

    q2r, q2i = rows_gp(qr[:, rev]), rows_gp(qi[:, rev])
    q2 = jnp.concatenate([q2r, q2r, q2i, q2i], axis=-1)
    er = ccr[:, None] * pwr[:, 1:cs + 1, None, :] - cci[:, None] * pwi[:, 1:cs + 1, None, :]
    ei = ccr[:, None] * pwi[:, 1:cs + 1, None, :] + cci[:, None] * pwr[:, 1:cs + 1, None, :]

    def lanes_gp(a):
        return a.reshape(noct, gpo, cs, p, n).transpose(0, 2, 4, 1, 3).reshape(noct, cs, n, gpo * p)

    ec = jnp.concatenate([lanes_gp(er), -lanes_gp(ei)], axis=2)
    mr, mi = powers(cs * (2.0 ** jnp.arange(nsteps, dtype=F32)))
    mr = mr.reshape(noct, gpo, nsteps, n).transpose(0, 2, 1, 3).reshape(noct, nsteps, gpo * n)
    mi = mi.reshape(noct, gpo, nsteps, n).transpose(0, 2, 1, 3).reshape(noct, nsteps, gpo * n)
    pad = ((0, 0), (0, (-nsteps) % SUBLANES), (0, 0))
    p1 = jnp.pad(jnp.concatenate([mr, mr], axis=-1), pad)
    p2 = jnp.pad(jnp.concatenate([-mi, mi], axis=-1), pad)
    d8 = jnp.tile(d.astype(F32).reshape(noct, 1, LANES), (1, 1, cs))
    return kt, q2, ec, p1, p2, d8


def _lru_kernel(x_ref, gate_ref, cw_ref, cb_ref, wh_ref, wl_ref, bias_ref, lam_ref, o_ref, halo_ref, h_ref):
    @pl.when(pl.program_id(1) == 0)
    def _():
        halo_ref[...] = jnp.zeros_like(halo_ref)
        h_ref[...] = jnp.zeros_like(h_ref)

    x = x_ref[...]
    rows, width = x.shape
    xc = _causal_conv(x, halo_ref[...], cw_ref[...]) + cb_ref[...]
    halo_ref[...] = x[rows - SUBLANES:rows]
    gates = _mm3_pre(xc, wh_ref[...], wl_ref[...]) + bias_ref[...]
    r = _sigmoid(gates[:, :width])
    i = _sigmoid(gates[:, width:])
    log_a = (-LRU_C) * r * _softplus(-lam_ref[...])
    a = jnp.exp(log_a)
    b = jnp.sqrt(-jnp.tanh(log_a) * (a * a + 1.0)) * (i * xc)
    row = lax.broadcasted_iota(jnp.int32, a.shape, 0)
    s = 1
    while s < rows:
        m = row >= s
        a_s = jnp.where(m, pltpu.roll(a, s, 0), 1.0)
        b_s = jnp.where(m, pltpu.roll(b, s, 0), 0.0)
        b = a * b_s + b
        a = a * a_s
        s *= 2
    h = a * h_ref[0:1] + b
    h_ref[0:1] = h[rows - 1:rows]
    o_ref[...] = h * _gelu_tanh(gate_ref[...])


def _lru_mix(proj, x_blk, gate_blk, cw, cb, wh, wl, bias, lam, layer, bsz, seq):
    width = cw.shape[-1]
    tl = min(LRU_TL, seq)
    nt = seq // tl

    def vec(rows):
        return pl.BlockSpec((None, rows, width), lambda b, c: (layer, 0, 0))

    return pl.pallas_call(
        _lru_kernel,
        grid=(bsz, nt),
        in_specs=[
            pl.BlockSpec((tl, width), lambda b, c: (b * nt + c, x_blk)),
            pl.BlockSpec((tl, width), lambda b, c: (b * nt + c, gate_blk)),
            vec(CONV_WIDTH), vec(1),
            pl.BlockSpec((None, width, 2 * width), lambda b, c: (layer, 0, 0)),
            pl.BlockSpec((None, width, 2 * width), lambda b, c: (layer, 0, 0)),
            pl.BlockSpec((None, 1, 2 * width), lambda b, c: (layer, 0, 0)),
            vec(1),
        ],
        out_specs=pl.BlockSpec((tl, width), lambda b, c: (b * nt + c, 0)),
        out_shape=jax.ShapeDtypeStruct((bsz * seq, width), F32),
        scratch_shapes=[pltpu.VMEM((SUBLANES, width), F32), pltpu.VMEM((SUBLANES, width), F32)],
        compiler_params=_cparams(("parallel", "arbitrary")),
        name="rglru",
    )(proj, proj, cw, cb, wh, wl, bias, lam)


def _parts(x, passes):
    if passes == 1:
        return (x.astype(BF16),)
    return _split2(x)


def _mmp(a, b, mm=_mm):
    out = mm(a[0], b[0])
    if len(a) > 1 and len(b) > 1:
        out = out + (mm(a[1], b[0]) + mm(a[0], b[1]))
    return out


def _tri_inverse_all(ms, eye, masks, passes):
    xs = [eye - jnp.where(masks[0], m, 0.0) for m in ms]
    for mk in masks[1:]:
        xp = [_parts(x, passes) for x in xs]
        mp = [_parts(jnp.where(mk, m, 0.0), passes) for m in ms]
        ys = [_mmp(a, b) for a, b in zip(xp, mp)]
        yp = [_parts(y, passes) for y in ys]
        xs = [x - _mmp(a, b) for x, a, b in zip(xs, yp, xp)]
    return xs


def _gdn_kernel(q_ref, k_ref, v_ref, z_ref, ab_ref, cq_ref, ck_ref, cv_ref, alog_ref, dtb_ref, ng_ref, o_ref,
                s_ref, hq_ref, hk_ref, hv_ref):
    @pl.when(pl.program_id(2) == 0)
    def _():
        s_ref[...] = jnp.zeros_like(s_ref)
        hq_ref[...] = jnp.zeros_like(hq_ref)
        hk_ref[...] = jnp.zeros_like(hk_ref)
        hv_ref[...] = jnp.zeros_like(hv_ref)

    rows = q_ref.shape[0]
    cs = GDN_CHUNK
    hd = GDN_HEAD_DIM
    nheads = GDN_HEADS_PER_STEP

    def conv_silu(x_ref, halo_ref, cw_ref):
        x = x_ref[...]
        y = _causal_conv(x, halo_ref[...], cw_ref[...])
        halo_ref[...] = x[rows - SUBLANES:rows]
        return _silu(y)

    q = conv_silu(q_ref, hq_ref, cq_ref)
    k = conv_silu(k_ref, hk_ref, ck_ref)
    v = conv_silu(v_ref, hv_ref, cv_ref)
    z = z_ref[...]

    ab = ab_ref[...]
    g_all = -jnp.exp(alog_ref[...]) * _softplus(ab + dtb_ref[...])
    beta_all = _sigmoid(ab)
    crow = lax.broadcasted_iota(jnp.int32, g_all.shape, 0) & (cs - 1)
    gc = g_all
    s = 1
    while s < cs:
        gc = gc + _shift_rows(gc, s, crow)
        s *= 2
    gc_t = gc.T

    ii = lax.broadcasted_iota(jnp.int32, (cs, cs), 0)
    jj = lax.broadcasted_iota(jnp.int32, (cs, cs), 1)
    lower = ii >= jj
    strict = ii > jj
    eye = jnp.where(ii == jj, 1.0, 0.0).astype(F32)
    masks = [strict & ((ii >> 1) == (jj >> 1))]
    lvl = 1
    while (1 << lvl) < cs:
        masks.append(strict & ((ii >> (lvl + 1)) == (jj >> (lvl + 1))) & ((ii >> lvl) != (jj >> lvl)))
        lvl += 1

    ng = ng_ref[...]
    nchunks = rows // cs
    heads = range(nheads)
    qn, kn = [], []
    for h in heads:
        hs = slice(h * hd, (h + 1) * hd)
        qh, kh = q[:, hs], k[:, hs]
        qn.append(qh * (lax.rsqrt(jnp.sum(qh * qh, axis=-1, keepdims=True) + RMS_EPS) * (hd ** -0.5)))
        kn.append(kh * lax.rsqrt(jnp.sum(kh * kh, axis=-1, keepdims=True) + RMS_EPS))

    probs = [(c, h) for c in range(nchunks) for h in heads]
    rs = lambda c: slice(c * cs, (c + 1) * cs)
    hsl = lambda h: slice(h * hd, (h + 1) * hd)
    qc = [qn[h][rs(c)] for c, h in probs]
    kc = [kn[h][rs(c)] for c, h in probs]
    bc = [jnp.broadcast_to(beta_all[rs(c), nheads + h:nheads + h + 1], (cs, hd)) for c, h in probs]
    gcc = [gc[rs(c), h:h + 1] for c, h in probs]
    dec = [jnp.exp(jnp.where(lower, gcc[p] - gc_t[h:h + 1, rs(c)], -jnp.inf)) for p, (c, h) in enumerate(probs)]
    kb = [a * b for a, b in zip(kc, bc)]
    vb = [v[rs(c), hsl(h)] * bc[p] for p, (c, h) in enumerate(probs)]
    eg = [jnp.broadcast_to(jnp.exp(g), (cs, hd)) for g in gcc]

    kcp = [_parts(a, GDN_SOLVE_PASSES) for a in kc]
    kbp = [_parts(a, GDN_SOLVE_PASSES) for a in kb]
    ms = [jnp.where(strict, _mmp(a, b, _mm_nt) * d, 0.0) for a, b, d in zip(kbp, kcp, dec)]
    tinv = _tri_inverse_all(ms, eye, masks, GDN_SOLVE_PASSES)
    rhs = [jnp.concatenate([a, b * e], axis=1) for a, b, e in zip(vb, kb, eg)]
    sol = [_mmp(_parts(t, GDN_SOLVE_PASSES), _parts(r, GDN_SOLVE_PASSES)) for t, r in zip(tinv, rhs)]
    qcb = [a.astype(BF16) for a in qc]
    qk = [jnp.where(lower, _mm_nt(a, b[0]) * d, 0.0).astype(BF16) for a, b, d in zip(qcb, kcp, dec)]
    qd = [(a * e).astype(BF16) for a, e in zip(qc, eg)]
    glast = [g[cs - 1:cs] for g in gcc]
    kd = [(a * jnp.exp(gl - g)).astype(BF16) for a, gl, g in zip(kc, glast, gcc)]

    state = [s_ref[h] for h in heads]
    for c in range(nchunks):
        ps = [c * nheads + h for h in heads]
        sb = [s.astype(BF16) for s in state]
        v_new = [sol[p][:, :hd] - _mm(sol[p][:, hd:].astype(BF16), sb[h]) for h, p in zip(heads, ps)]
        vnb = [a.astype(BF16) for a in v_new]
        outs = [_mm(qd[p], sb[h]) + _mm(qk[p], vnb[h]) for h, p in zip(heads, ps)]
        state = [state[h] * jnp.exp(glast[p]) + _mm_tn(kd[p], vnb[h]) for h, p in zip(heads, ps)]
        for h, o in zip(heads, outs):
            o = o * lax.rsqrt(jnp.mean(o * o, axis=-1, keepdims=True) + RMS_EPS) * ng
            o_ref[rs(c), hsl(h)] = o * _silu(z[rs(c), hsl(h)])
    for h in heads:
        s_ref[h] = state[h]


def _gdn_mix(proj, blk, conv_w, alog, dtb, ng, layer, bsz, seq, n_heads):
    hps = GDN_HEADS_PER_STEP
    wblk = hps * GDN_HEAD_DIM
    ngrp = n_heads // hps
    tb = min(GDN_TB, seq)
    nt = seq // tb

    def tok(base):
        return pl.BlockSpec((tb, wblk), lambda b, g, t: (b * nt + t, base + g))

    def cw(base):
        return pl.BlockSpec((None, CONV_WIDTH, wblk), lambda b, g, t: (layer, 0, base + g))

    def vec128():
        return pl.BlockSpec((None, 1, LANES), lambda b, g, t: (layer * ngrp + g, 0, 0))

    return pl.pallas_call(
        _gdn_kernel,
        grid=(bsz, ngrp, nt),
        in_specs=[
            tok(blk['q']), tok(blk['k']), tok(blk['v']), tok(blk['z']),
            pl.BlockSpec((tb, LANES), lambda b, g, t: (b * nt + t, blk['ab'] + g)),
            cw(0), cw(ngrp), cw(2 * ngrp),
            vec128(), vec128(),
            pl.BlockSpec((None, 1, GDN_HEAD_DIM), lambda b, g, t: (layer, 0, 0)),
        ],
        out_specs=pl.BlockSpec((tb, wblk), lambda b, g, t: (b * nt + t, g)),
        out_shape=jax.ShapeDtypeStruct((bsz * seq, n_heads * GDN_HEAD_DIM), F32),
        scratch_shapes=[
            pltpu.VMEM((hps, GDN_HEAD_DIM, GDN_HEAD_DIM), F32),
            pltpu.VMEM((SUBLANES, wblk), F32), pltpu.VMEM((SUBLANES, wblk), F32), pltpu.VMEM((SUBLANES, wblk), F32),
        ],
        compiler_params=_cparams(("parallel", "parallel", "arbitrary")),
        name="gdn",
    )(proj, proj, proj, proj, proj, conv_w, conv_w, conv_w, alog, dtb, ng)


def _out_kernel(alpha, x_ref, ys_ref, yg0_ref, yg1_ref, yl_ref, wglu_ref, bglu_ref, w0_ref, w1_ref, w2_ref, w3_ref,
                g_ref, b_ref, o_ref):
    ys = _gelu_tanh(jnp.concatenate([ys_ref[o] for o in range(ys_ref.shape[0])], axis=1))
    ys = ys * _sigmoid(_mm(ys.astype(BF16), wglu_ref[...]) + bglu_ref[...])
    mixed = _mm(ys.astype(BF16), w0_ref[...])
    mixed += _mm(yg0_ref[...].astype(BF16), w1_ref[...])
    mixed += _mm(yg1_ref[...].astype(BF16), w2_ref[...])
    mixed += _mm(yl_ref[...].astype(BF16), w3_ref[...])
    o_ref[...] = _layer_norm(alpha * x_ref[...] + mixed, g_ref[...], b_ref[...])


def _out_proj_ln(x, y_s5, y_gdn, y_lru, wglu, bglu, w_out, g, b, layer, alpha):
    t, d = x.shape
    noct = y_s5.shape[0]
    wb = noct * y_s5.shape[-1]
    tm = min(OUT_TM, t)

    def act(col):
        return pl.BlockSpec((tm, wb), lambda i: (i, col))

    def wrow(r):
        return pl.BlockSpec((None, wb, d), lambda i: (layer, r, 0))

    def vec(n):
        return pl.BlockSpec((None, 1, n), lambda i: (layer, 0, 0))

    return pl.pallas_call(
        functools.partial(_out_kernel, alpha),
        grid=(t // tm,),
        in_specs=[
            pl.BlockSpec((tm, d), lambda i: (i, 0)),
            pl.BlockSpec((noct, tm, y_s5.shape[-1]), lambda i: (0, i, 0)),
            act(0), act(1), act(0),
            pl.BlockSpec((None, wb, wb), lambda i: (layer, 0, 0)), vec(wb),
            wrow(0), wrow(1), wrow(2), wrow(3),
            vec(d), vec(d),
        ],
        out_specs=pl.BlockSpec((tm, d), lambda i: (i, 0)),
        out_shape=jax.ShapeDtypeStruct((t, d), F32),
        compiler_params=_cparams(("parallel",)),
        name="out_proj_ln",
    )(x, y_s5, y_gdn, y_gdn, y_lru, wglu, bglu, w_out, w_out, w_out, w_out, g, b)


def _row3(p):
    return p.astype(F32).reshape(p.shape[0], 1, p.shape[1])


def kernel(x, ffn1_w_gate, ffn1_w_up, ffn1_w_down, ln1_g, ln1_b, w_in, s5_lambda_re, s5_lambda_im, s5_b_re, s5_b_im,
           s5_c_re, s5_c_im, s5_d, s5_log_step, s5_w_glu, s5_b_glu, gdn_conv_w, gdn_a_log, gdn_dt_bias, gdn_norm_g,
           lru_conv_w, lru_conv_b, lru_w_a, lru_b_a, lru_w_x, lru_b_x, lru_lambda, w_out, ln2_g, ln2_b, ffn2_w_gate,
           ffn2_w_up, ffn2_w_down, ln3_g, ln3_b):
    bsz, seq, d_model = x.shape
    depth = w_in.shape[0]
    t = bsz * seq
    alpha = (2.0 * depth) ** 0.25
    s5_w = s5_d.shape[-1]
    n_heads = gdn_a_log.shape[-1]
    gdn_w = n_heads * GDN_HEAD_DIM
    lru_w = lru_lambda.shape[-1]
    lru_blocks, lru_blk = lru_w_a.shape[1], lru_w_a.shape[2]
    hps = GDN_HEADS_PER_STEP
    ngrp = n_heads // hps
    wblk = hps * GDN_HEAD_DIM
    assert s5_w == PROJ_BLOCK and lru_w == PROJ_BLOCK and gdn_w % wblk == 0 and (4 * gdn_w) % PROJ_BLOCK == 0

    o_qkv = s5_w
    o_a = o_qkv + 4 * gdn_w
    o_b = o_a + n_heads
    o_lx = o_b + n_heads
    main = jnp.concatenate([w_in[..., o_qkv:o_a], w_in[..., o_lx:o_lx + 2 * lru_w]], axis=-1)
    w_u = w_in[..., :s5_w].astype(BF16)
    ab_cols = []
    for g in range(ngrp):
        wa = w_in[..., o_a + g * hps:o_a + (g + 1) * hps]
        wb_ = w_in[..., o_b + g * hps:o_b + (g + 1) * hps]
        ab_cols += [wa, wb_, jnp.zeros(w_in.shape[:2] + (LANES - 2 * hps,), w_in.dtype)]
    n_main = main.shape[-1]
    n_proj = -(-(n_main + ngrp * LANES) // PROJ_TN) * PROJ_TN
    pad = jnp.zeros(w_in.shape[:2] + (n_proj - n_main - ngrp * LANES,), w_in.dtype)
    w_in_p = jnp.concatenate([main] + ab_cols + [pad], axis=-1).astype(BF16)
    blk = {'q': 0, 'k': gdn_w // wblk, 'v': 2 * gdn_w // wblk, 'z': 3 * gdn_w // wblk, 'ab': n_main // LANES}
    lx_blk = (4 * gdn_w) // PROJ_BLOCK
    lg_blk = lx_blk + 1

    def ab_lanes(p):
        pg = p.astype(F32).reshape(depth, ngrp, hps)
        out = jnp.concatenate([pg, pg, jnp.zeros((depth, ngrp, LANES - 2 * hps), F32)], axis=-1)
        return out.reshape(depth * ngrp, 1, LANES)

    alog_l = ab_lanes(gdn_a_log)
    dtb_l = ab_lanes(gdn_dt_bias)

    eye_b = jnp.eye(lru_blocks, dtype=F32)

    def blockdiag(w):
        return jnp.einsum('lhij,hg->lhigj', w.astype(F32), eye_b).reshape(depth, lru_w, lru_w)

    lru_w2 = jnp.concatenate([blockdiag(lru_w_a), blockdiag(lru_w_x)], axis=-1)
    lru_wh, lru_wl = _split2(lru_w2)
    lru_bias = jnp.concatenate([lru_b_a, lru_b_x], axis=-1).astype(F32).reshape(depth, 1, 2 * lru_w)

    bf = lambda w: w.astype(BF16)
    f1g, f1u, f1d = bf(ffn1_w_gate), bf(ffn1_w_up), bf(ffn1_w_down)
    f2g, f2u, f2d = bf(ffn2_w_gate), bf(ffn2_w_up), bf(ffn2_w_down)
    w_out_b = bf(w_out)
    wglu_b = bf(s5_w_glu)

    s5_steps = int(math.log2(min(S5_ROWS, seq // S5_CHUNK)))
    s5_mats = jax.vmap(functools.partial(_s5_matrices, nsteps=s5_steps))(
        s5_lambda_re, s5_lambda_im, s5_b_re, s5_b_im, s5_c_re, s5_c_im, s5_d, s5_log_step)

    h = x.reshape(t, d_model).astype(F32)
    for l in range(depth):
        h = _ffn_ln(h, f1g, f1u, f1d, _row3(ln1_g), _row3(ln1_b), l, alpha)
        proj, u8 = _in_proj(h, w_in_p, w_u, l)
        y_s5 = _s5_core(u8, s5_mats, l, seq)
        y_gdn = _gdn_mix(proj, blk, gdn_conv_w.astype(F32), alog_l, dtb_l, _row3(gdn_norm_g), l, bsz, seq, n_heads)
        y_lru = _lru_mix(proj, lx_blk, lg_blk, lru_conv_w.astype(F32), _row3(lru_conv_b), lru_wh, lru_wl, lru_bias,
                         _row3(lru_lambda), l, bsz, seq)
        h = _out_proj_ln(h, y_s5, y_gdn, y_lru, wglu_b, _row3(s5_b_glu), w_out_b, _row3(ln2_g), _row3(ln2_b), l,
                         alpha)
        h = _ffn_ln(h, f2g, f2u, f2d, _row3(ln3_g), _row3(ln3_b), l, alpha)
    return h.reshape(bsz, seq, d_model).astype(x.dtype)
```

```python
import functools
import math

import jax
import jax.numpy as jnp
from jax import lax
from jax.experimental import pallas as pl
from jax.experimental.pallas import tpu as pltpu

F32 = jnp.float32
BF16 = jnp.bfloat16

S5_GROUP = 16
S5_STATE = 64
S5_EIG_CLIP = -1e-4
S5_CHUNK = 16
S5_ROWS = 256
S5_PASSES = 1
GDN_HEAD_DIM = 128
GDN_CHUNK = 64
GDN_HEADS_PER_STEP = 8
GDN_SOLVE_PASSES = 1
LRU_C = 8.0
CONV_WIDTH = 4
LN_EPS = 1e-5
RMS_EPS = 1e-6
MACARON_WEIGHT = 0.5

LANES = 128
SUBLANES = 8
VMEM_LIMIT_BYTES = 56 * 1024 * 1024

FFN_TM = 1024
FFN_TF = 256
PROJ_TM = 1024
PROJ_TN = 768
OUT_TM = 512
LRU_TL = 512
GDN_TB = 128
PROJ_BLOCK = 512


def _cparams(sem):
    return pltpu.CompilerParams(dimension_semantics=sem, vmem_limit_bytes=VMEM_LIMIT_BYTES)


def _split2(x):
    hi = x.astype(BF16)
    lo = (x - hi.astype(F32)).astype(BF16)
    return hi, lo


def _mm(a, b):
    return jnp.dot(a, b, preferred_element_type=F32)


def _mm_nt(a, b):
    return lax.dot_general(a, b, (((1,), (1,)), ((), ())), preferred_element_type=F32)


def _mm_tn(a, b):
    return lax.dot_general(a, b, (((0,), (0,)), ((), ())), preferred_element_type=F32)


def _mm3(a, b):
    ah, al = _split2(a)
    bh, bl = _split2(b)
    return _mm(ah, bh) + (_mm(al, bh) + _mm(ah, bl))


def _mm3_nt(a, b):
    ah, al = _split2(a)
    bh, bl = _split2(b)
    return _mm_nt(ah, bh) + (_mm_nt(al, bh) + _mm_nt(ah, bl))


def _mm3_pre(a, bh, bl):
    ah, al = _split2(a)
    return _mm(ah, bh) + (_mm(al, bh) + _mm(ah, bl))


def _layer_norm(y, g, b):
    mu = jnp.mean(y, axis=-1, keepdims=True)
    yc = y - mu
    var = jnp.mean(yc * yc, axis=-1, keepdims=True)
    return yc * lax.rsqrt(var + LN_EPS) * g + b


def _sigmoid(x):
    return 0.5 * jnp.tanh(0.5 * x) + 0.5


def _silu(x):
    return x * _sigmoid(x)


def _gelu_tanh(x):
    c = math.sqrt(2.0 / math.pi)
    return 0.5 * x * (1.0 + jnp.tanh(c * (x + 0.044715 * (x * x * x))))


def _softplus(x):
    return jnp.maximum(x, 0.0) + jnp.log1p(jnp.exp(-jnp.abs(x)))


def _shift_rows(x, s, row):
    return jnp.where(row >= s, pltpu.roll(x, s, 0), 0.0)


def _causal_conv(x, halo, cw):
    return _causal_conv_padded(jnp.concatenate([halo, x], axis=0), SUBLANES, cw)


def _causal_conv_padded(xp, lead, cw):
    y = cw[CONV_WIDTH - 1:CONV_WIDTH] * xp[lead:]
    for s in range(1, CONV_WIDTH):
        k = CONV_WIDTH - 1 - s
        y = y + cw[k:k + 1] * pltpu.roll(xp, s, 0)[lead:]
    return y


def _ffn_kernel(alpha, x_ref, wgu_ref, wd_ref, g_ref, b_ref, o_ref, xb_ref):
    j = pl.program_id(1)
    tf = wd_ref.shape[0]

    @pl.when(j == 0)
    def _():
        x = x_ref[...]
        xb_ref[...] = x.astype(BF16)
        o_ref[...] = (alpha / MACARON_WEIGHT) * x

    gu = _mm(xb_ref[...], wgu_ref[...])
    h = (_silu(gu[:, :tf]) * gu[:, tf:]).astype(BF16)
    o_ref[...] += _mm(h, wd_ref[...])

    @pl.when(j == pl.num_programs(1) - 1)
    def _():
        o_ref[...] = _layer_norm(MACARON_WEIGHT * o_ref[...], g_ref[...], b_ref[...])


def _ffn_gate_up(wg, wu):
    depth, d, f = wg.shape
    tf = min(FFN_TF, f)
    both = jnp.stack([wg.reshape(depth, d, f // tf, tf), wu.reshape(depth, d, f // tf, tf)], axis=3)
    return both.reshape(depth, d, 2 * f).astype(BF16)


def _ffn_ln(x, wgu, wd, g, b, layer, alpha):
    t, d = x.shape
    f = wd.shape[1]
    tm, tf = min(FFN_TM, t), min(FFN_TF, f)
    return pl.pallas_call(
        functools.partial(_ffn_kernel, alpha),
        grid=(t // tm, f // tf),
        in_specs=[
            pl.BlockSpec((tm, d), lambda i, j: (i, 0)),
            pl.BlockSpec((None, d, 2 * tf), lambda i, j: (layer, 0, j)),
            pl.BlockSpec((None, tf, d), lambda i, j: (layer, j, 0)),
            pl.BlockSpec((None, 1, d), lambda i, j: (layer, 0, 0)),
            pl.BlockSpec((None, 1, d), lambda i, j: (layer, 0, 0)),
        ],
        out_specs=pl.BlockSpec((tm, d), lambda i, j: (i, 0)),
        out_shape=jax.ShapeDtypeStruct((t, d), F32),
        scratch_shapes=[pltpu.VMEM((tm, d), BF16)],
        compiler_params=_cparams(("parallel", "arbitrary")),
        name="ffn_ln",
    )(x, wgu, wd, g, b)


def _proj_kernel(x_ref, w_ref, wu_ref, o_ref, u_ref, xb_ref):
    @pl.when(pl.program_id(1) == 0)
    def _():
        xb = x_ref[...].astype(BF16)
        xb_ref[...] = xb
        u = _mm(xb, wu_ref[...])
        for o in range(u_ref.shape[0]):
            u_ref[o] = u[:, o * LANES:(o + 1) * LANES]

    o_ref[...] = _mm(xb_ref[...], w_ref[...])


def _in_proj(x, w, w_u, layer):
    t, d = x.shape
    n = w.shape[-1]
    nu = w_u.shape[-1]
    tm, tn = min(PROJ_TM, t), PROJ_TN
    return pl.pallas_call(
        _proj_kernel,
        grid=(t // tm, n // tn),
        in_specs=[
            pl.BlockSpec((tm, d), lambda i, j: (i, 0)),
            pl.BlockSpec((None, d, tn), lambda i, j: (layer, 0, j)),
            pl.BlockSpec((None, d, nu), lambda i, j: (layer, 0, 0)),
        ],
        out_specs=[
            pl.BlockSpec((tm, tn), lambda i, j: (i, j)),
            pl.BlockSpec((nu // LANES, tm, LANES), lambda i, j: (0, i, 0)),
        ],
        out_shape=[jax.ShapeDtypeStruct((t, n), F32), jax.ShapeDtypeStruct((nu // LANES, t, LANES), F32)],
        scratch_shapes=[pltpu.VMEM((tm, d), BF16)],
        compiler_params=_cparams(("parallel", "arbitrary")),
        name="in_proj",
    )(x, w, w_u)


def _s5_expand_operators(k_ref, q_ref, ec_ref, t_ref, g_ref, e_ref):
    cs = k_ref.shape[0]
    n = ec_ref.shape[1] // 2
    gpo = LANES // S5_GROUP
    rgrp = lax.broadcasted_iota(jnp.int32, (LANES, LANES), 0) // S5_GROUP
    lane = lax.broadcasted_iota(jnp.int32, (LANES, LANES), 1)
    lgrp = lane // S5_GROUP
    zero = jnp.zeros((LANES, LANES), BF16)
    bd = [jnp.where(rgrp == lgrp, k_ref[tau], 0.0).astype(BF16) for tau in range(cs)]
    for i in range(cs):
        for j in range(cs):
            t_ref[i * LANES:(i + 1) * LANES, j * LANES:(j + 1) * LANES] = bd[j - i] if j >= i else zero
    lhalf = lane // n
    per_tile = LANES // n
    for i in range(cs):
        q2 = q_ref[i]
        for tile in range(2 * gpo // per_tile):
            c, k = divmod(tile, gpo // per_tile)
            x = q2[:, c * LANES:(c + 1) * LANES]
            g_ref[i * LANES:(i + 1) * LANES, tile * LANES:(tile + 1) * LANES] = jnp.where(
                rgrp == per_tile * k + lhalf, x, 0.0).astype(BF16)
    lgrp_n = lax.broadcasted_iota(jnp.int32, (n, LANES), 1) // S5_GROUP
    for j in range(cs):
        e = ec_ref[j]
        for c in range(2):
            ec = e[c * n:(c + 1) * n]
            for h in range(gpo):
                r0 = (c * gpo + h) * n
                e_ref[r0:r0 + n, j * LANES:(j + 1) * LANES] = jnp.where(lgrp_n == h, ec, 0.0).astype(BF16)


def _s5_kernel(blocks_per_seq, u_ref, k_ref, q_ref, ec_ref, p1_ref, p2_ref, d_ref, o_ref, c_ref, t_ref, g_ref,
               e_ref):
    @pl.when(pl.program_id(1) == 0)
    def _():
        _s5_expand_operators(k_ref, q_ref, ec_ref, t_ref, g_ref, e_ref)

    @pl.when(pl.program_id(1) % blocks_per_seq == 0)
    def _():
        c_ref[...] = jnp.zeros_like(c_ref)

    nrow = u_ref.shape[0] // S5_CHUNK
    u = jnp.concatenate([u_ref[pl.ds(i, nrow, stride=S5_CHUNK), :] for i in range(S5_CHUNK)], axis=1)
    up = _parts(u, S5_PASSES)
    wt, wg, we = t_ref[...], g_ref[...], e_ref[...]
    y = d_ref[...] * u
    z = None
    for a in up:
        y = y + _mm(a, wt)
        z = _mm(a, wg) if z is None else z + _mm(a, wg)
    rows = z.shape[0]
    half = z.shape[1] // 2
    p1 = p1_ref[...]
    p2 = p2_ref[...]
    row = lax.broadcasted_iota(jnp.int32, z.shape, 0)
    carry = c_ref[0:1]
    z = z + jnp.where(row == 0, carry * p1[0:1] + pltpu.roll(carry, half, 1) * p2[0:1], 0.0)
    s, k = 1, 0
    while s < rows:
        zs = _shift_rows(z, s, row)
        z = z + (zs * p1[k:k + 1] + pltpu.roll(zs, half, 1) * p2[k:k + 1])
        s *= 2
        k += 1
    h_in = jnp.where(row == 0, carry, pltpu.roll(z, 1, 0))
    c_ref[0:1] = z[rows - 1:rows]
    for a in _parts(h_in, S5_PASSES):
        y = y + _mm(a, we)
    for i in range(S5_CHUNK):
        o_ref[pl.ds(i, nrow, stride=S5_CHUNK), :] = y[:, i * LANES:(i + 1) * LANES]


def _s5_core(u8, mats, layer, seq):
    kt, q2, ec, p1, p2, d8 = mats
    noct, t, lanes = u8.shape
    width = S5_CHUNK * lanes
    rows = t // S5_CHUNK
    nstate = p1.shape[-1]
    rb = min(S5_ROWS, seq // S5_CHUNK)
    tok = pl.BlockSpec((None, rb * S5_CHUNK, lanes), lambda o, r: (o, r, 0))

    def spec(a):
        return pl.BlockSpec((None, None) + a.shape[2:], lambda o, r: (layer, o) + (0,) * (a.ndim - 2))

    return pl.pallas_call(
        functools.partial(_s5_kernel, (seq // S5_CHUNK) // rb),
        grid=(noct, rows // rb),
        in_specs=[tok, spec(kt), spec(q2), spec(ec), spec(p1), spec(p2), spec(d8)],
        out_specs=tok,
        out_shape=jax.ShapeDtypeStruct((noct, t, lanes), F32),
        scratch_shapes=[pltpu.VMEM((SUBLANES, nstate), F32), pltpu.VMEM((width, width), BF16),
                        pltpu.VMEM((width, nstate), BF16), pltpu.VMEM((nstate, width), BF16)],
        compiler_params=_cparams(("parallel", "arbitrary")),
        name="s5_core",
    )(u8, kt, q2, ec, p1, p2, d8)


def _s5_matrices(lam_re, lam_im, b_re, b_im, c_re, c_im, d, log_step, nsteps):
    groups, n = lam_re.shape
    p = S5_GROUP
    cs = S5_CHUNK
    gpo = LANES // p
    noct = groups // gpo
    hi = lax.Precision.HIGHEST
    lr = jnp.minimum(lam_re.astype(F32), S5_EIG_CLIP)
    li = lam_im.astype(F32)
    dt = jnp.exp(log_step.astype(F32))[:, None]
    zr, zi = lr * dt, li * dt

    def powers(ex):
        mag = jnp.exp(ex[None, :, None] * zr[:, None, :])
        ang = ex[None, :, None] * zi[:, None, :]
        return mag * jnp.cos(ang), mag * jnp.sin(ang)

    pwr, pwi = powers(jnp.arange(cs + 1, dtype=F32))
    ar, ai = pwr[:, 1] - 1.0, pwi[:, 1]
    den = lr * lr + li * li
    cr, ci = (ar * lr + ai * li) / den, (ai * lr - ar * li) / den
    br, bi = b_re.astype(F32), b_im.astype(F32)
    bbr = cr[..., None] * br - ci[..., None] * bi
    bbi = cr[..., None] * bi + ci[..., None] * br
    ccr, cci = c_re.astype(F32), c_im.astype(F32)
    qr = pwr[:, :cs, :, None] * bbr[:, None] - pwi[:, :cs, :, None] * bbi[:, None]
    qi = pwr[:, :cs, :, None] * bbi[:, None] + pwi[:, :cs, :, None] * bbr[:, None]
    kmat = (jnp.einsum('gpn,gtnq->gtpq', ccr, qr, precision=hi)
            - jnp.einsum('gpn,gtnq->gtpq', cci, qi, precision=hi))
    kt = kmat.reshape(noct, gpo, cs, p, p).transpose(0, 2, 1, 4, 3).reshape(noct, cs, gpo * p, p)
    kt = jnp.tile(kt, (1, 1, 1, gpo))
    rev = cs - 1 - jnp.arange(cs)

    def rows_gp(a):
        return a.reshape(noct, gpo, cs, n, p).transpose(0, 2, 1, 4, 3).reshape(noct, cs, gpo * p, n)

    q2r, q2i = rows_gp(qr[:, rev]), rows_gp(qi[:, rev])
    q2 = jnp.concatenate([q2r, q2r, q2i, q2i], axis=-1)
    er = ccr[:, None] * pwr[:, 1:cs + 1, None, :] - cci[:, None] * pwi[:, 1:cs + 1, None, :]
    ei = ccr[:, None] * pwi[:, 1:cs + 1, None, :] + cci[:, None] * pwr[:, 1:cs + 1, None, :]

    def lanes_gp(a):
        return a.reshape(noct, gpo, cs, p, n).transpose(0, 2, 4, 1, 3).reshape(noct, cs, n, gpo * p)

    ec = jnp.concatenate([lanes_gp(er), -lanes_gp(ei)], axis=2)
    mr, mi = powers(cs * (2.0 ** jnp.arange(nsteps, dtype=F32)))
    mr = mr.reshape(noct, gpo, nsteps, n).transpose(0, 2, 1, 3).reshape(noct, nsteps, gpo * n)
    mi = mi.reshape(noct, gpo, nsteps, n).transpose(0, 2, 1, 3).reshape(noct, nsteps, gpo * n)
    pad = ((0, 0), (0, (-nsteps) % SUBLANES), (0, 0))
    p1 = jnp.pad(jnp.concatenate([mr, mr], axis=-1), pad)
    p2 = jnp.pad(jnp.concatenate([-mi, mi], axis=-1), pad)
    d8 = jnp.tile(d.astype(F32).reshape(noct, 1, LANES), (1, 1, cs))
    return kt, q2, ec, p1, p2, d8


def _lru_kernel(x_ref, gate_ref, cw_ref, cb_ref, wh_ref, wl_ref, bias_ref, lam_ref, o_ref, halo_ref, h_ref):
    @pl.when(pl.program_id(1) == 0)
    def _():
        halo_ref[...] = jnp.zeros_like(halo_ref)
        h_ref[...] = jnp.zeros_like(h_ref)

    x = x_ref[...]
    rows, width = x.shape
    xc = _causal_conv(x, halo_ref[...], cw_ref[...]) + cb_ref[...]
    halo_ref[...] = x[rows - SUBLANES:rows]
    gates = _mm3_pre(xc, wh_ref[...], wl_ref[...]) + bias_ref[...]
    r = _sigmoid(gates[:, :width])
    i = _sigmoid(gates[:, width:])
    log_a = (-LRU_C) * r * _softplus(-lam_ref[...])
    a = jnp.exp(log_a)
    b = jnp.sqrt(-jnp.tanh(log_a) * (a * a + 1.0)) * (i * xc)
    srow = lax.broadcasted_iota(jnp.int32, a.shape, 0) & (SUBLANES - 1)
    s = 1
    while s < SUBLANES:
        m = srow >= s
        a_s = jnp.where(m, pltpu.roll(a, s, 0), 1.0)
        b_s = jnp.where(m, pltpu.roll(b, s, 0), 0.0)
        b = a * b_s + b
        a = a * a_s
        s *= 2
    carry = h_ref[0:1]
    tiles = []
    for r in range(rows // SUBLANES):
        sl = slice(r * SUBLANES, (r + 1) * SUBLANES)
        h_tile = a[sl] * carry + b[sl]
        carry = h_tile[SUBLANES - 1:SUBLANES]
        tiles.append(h_tile)
    h_ref[0:1] = carry
    o_ref[...] = jnp.concatenate(tiles, axis=0) * _gelu_tanh(gate_ref[...])


def _lru_mix(proj, x_blk, gate_blk, cw, cb, wh, wl, bias, lam, layer, bsz, seq):
    width = cw.shape[-1]
    tl = min(LRU_TL, seq)
    nt = seq // tl

    def vec(rows):
        return pl.BlockSpec((None, rows, width), lambda b, c: (layer, 0, 0))

    return pl.pallas_call(
        _lru_kernel,
        grid=(bsz, nt),
        in_specs=[
            pl.BlockSpec((tl, width), lambda b, c: (b * nt + c, x_blk)),
            pl.BlockSpec((tl, width), lambda b, c: (b * nt + c, gate_blk)),
            vec(CONV_WIDTH), vec(1),
            pl.BlockSpec((None, width, 2 * width), lambda b, c: (layer, 0, 0)),
            pl.BlockSpec((None, width, 2 * width), lambda b, c: (layer, 0, 0)),
            pl.BlockSpec((None, 1, 2 * width), lambda b, c: (layer, 0, 0)),
            vec(1),
        ],
        out_specs=pl.BlockSpec((tl, width), lambda b, c: (b * nt + c, 0)),
        out_shape=jax.ShapeDtypeStruct((bsz * seq, width), F32),
        scratch_shapes=[pltpu.VMEM((SUBLANES, width), F32), pltpu.VMEM((SUBLANES, width), F32)],
        compiler_params=_cparams(("parallel", "arbitrary")),
        name="rglru",
    )(proj, proj, cw, cb, wh, wl, bias, lam)


def _parts(x, passes):
    if passes == 1:
        return (x.astype(BF16),)
    return _split2(x)


def _mmp(a, b, mm=_mm):
    out = mm(a[0], b[0])
    if len(a) > 1 and len(b) > 1:
        out = out + (mm(a[1], b[0]) + mm(a[0], b[1]))
    return out


def _tri_inverse_all(ms, eye, masks, passes):
    xs = [eye - jnp.where(masks[0], m, 0.0) for m in ms]
    for mk in masks[1:]:
        xp = [_parts(x, passes) for x in xs]
        mp = [_parts(jnp.where(mk, m, 0.0), passes) for m in ms]
        ys = [_mmp(a, b) for a, b in zip(xp, mp)]
        yp = [_parts(y, passes) for y in ys]
        xs = [x - _mmp(a, b) for x, a, b in zip(xs, yp, xp)]
    return xs


def _gdn_kernel(q_ref, k_ref, v_ref, z_ref, ab_ref, cq_ref, ck_ref, cv_ref, alog_ref, dtb_ref, ng_ref, o_ref,
                s_ref, hq_ref, hk_ref, hv_ref):
    @pl.when(pl.program_id(2) == 0)
    def _():
        s_ref[...] = jnp.zeros_like(s_ref)
        hq_ref[...] = jnp.zeros_like(hq_ref)
        hk_ref[...] = jnp.zeros_like(hk_ref)
        hv_ref[...] = jnp.zeros_like(hv_ref)

    rows = q_ref.shape[0]
    cs = GDN_CHUNK
    hd = GDN_HEAD_DIM
    nheads = GDN_HEADS_PER_STEP

    def conv_silu(x_ref, halo_ref, cw_ref):
        x = x_ref[...]
        y = _causal_conv(x, halo_ref[...], cw_ref[...])
        halo_ref[...] = x[rows - SUBLANES:rows]
        return _silu(y)

    q = conv_silu(q_ref, hq_ref, cq_ref)
    k = conv_silu(k_ref, hk_ref, ck_ref)
    v = conv_silu(v_ref, hv_ref, cv_ref)
    z = z_ref[...]

    ab = ab_ref[...]
    g_all = -jnp.exp(alog_ref[...]) * _softplus(ab + dtb_ref[...])
    beta_all = _sigmoid(ab)
    crow = lax.broadcasted_iota(jnp.int32, g_all.shape, 0) & (cs - 1)
    gc = g_all
    s = 1
    while s < cs:
        gc = gc + _shift_rows(gc, s, crow)
        s *= 2
    gc_t = gc.T

    ii = lax.broadcasted_iota(jnp.int32, (cs, cs), 0)
    jj = lax.broadcasted_iota(jnp.int32, (cs, cs), 1)
    lower = ii >= jj
    strict = ii > jj
    eye = jnp.where(ii == jj, 1.0, 0.0).astype(F32)
    masks = [strict & ((ii >> 1) == (jj >> 1))]
    lvl = 1
    while (1 << lvl) < cs:
        masks.append(strict & ((ii >> (lvl + 1)) == (jj >> (lvl + 1))) & ((ii >> lvl) != (jj >> lvl)))
        lvl += 1

    ng = ng_ref[...]
    nchunks = rows // cs
    heads = range(nheads)
    qn, kn = [], []
    for h in heads:
        hs = slice(h * hd, (h + 1) * hd)
        qh, kh = q[:, hs], k[:, hs]
        qn.append(qh * (lax.rsqrt(jnp.sum(qh * qh, axis=-1, keepdims=True) + RMS_EPS) * (hd ** -0.5)))
        kn.append(kh * lax.rsqrt(jnp.sum(kh * kh, axis=-1, keepdims=True) + RMS_EPS))

    probs = [(c, h) for c in range(nchunks) for h in heads]
    rs = lambda c: slice(c * cs, (c + 1) * cs)
    hsl = lambda h: slice(h * hd, (h + 1) * hd)
    qc = [qn[h][rs(c)] for c, h in probs]
    kc = [kn[h][rs(c)] for c, h in probs]
    bc = [jnp.broadcast_to(beta_all[rs(c), nheads + h:nheads + h + 1], (cs, hd)) for c, h in probs]
    gcc = [gc[rs(c), h:h + 1] for c, h in probs]
    dec = [jnp.exp(jnp.where(lower, gcc[p] - gc_t[h:h + 1, rs(c)], -jnp.inf)) for p, (c, h) in enumerate(probs)]
    kb = [a * b for a, b in zip(kc, bc)]
    vb = [v[rs(c), hsl(h)] * bc[p] for p, (c, h) in enumerate(probs)]
    eg = [jnp.broadcast_to(jnp.exp(g), (cs, hd)) for g in gcc]

    kcp = [_parts(a, GDN_SOLVE_PASSES) for a in kc]
    kbp = [_parts(a, GDN_SOLVE_PASSES) for a in kb]
    ms = [jnp.where(strict, _mmp(a, b, _mm_nt) * d, 0.0) for a, b, d in zip(kbp, kcp, dec)]
    tinv = _tri_inverse_all(ms, eye, masks, GDN_SOLVE_PASSES)
    rhs = [jnp.concatenate([a, b * e], axis=1) for a, b, e in zip(vb, kb, eg)]
    sol = [_mmp(_parts(t, GDN_SOLVE_PASSES), _parts(r, GDN_SOLVE_PASSES)) for t, r in zip(tinv, rhs)]
    qcb = [a.astype(BF16) for a in qc]
    qk = [jnp.where(lower, _mm_nt(a, b[0]) * d, 0.0).astype(BF16) for a, b, d in zip(qcb, kcp, dec)]
    qd = [(a * e).astype(BF16) for a, e in zip(qc, eg)]
    glast = [g[cs - 1:cs] for g in gcc]
    kd = [(a * jnp.exp(gl - g)).astype(BF16) for a, gl, g in zip(kc, glast, gcc)]

    state = [s_ref[h] for h in heads]
    for c in range(nchunks):
        ps = [c * nheads + h for h in heads]
        sb = [s.astype(BF16) for s in state]
        v_new = [sol[p][:, :hd] - _mm(sol[p][:, hd:].astype(BF16), sb[h]) for h, p in zip(heads, ps)]
        vnb = [a.astype(BF16) for a in v_new]
        outs = [_mm(qd[p], sb[h]) + _mm(qk[p], vnb[h]) for h, p in zip(heads, ps)]
        state = [state[h] * jnp.exp(glast[p]) + _mm_tn(kd[p], vnb[h]) for h, p in zip(heads, ps)]
        for h, o in zip(heads, outs):
            o = o * lax.rsqrt(jnp.mean(o * o, axis=-1, keepdims=True) + RMS_EPS) * ng
            o_ref[rs(c), hsl(h)] = o * _silu(z[rs(c), hsl(h)])
    for h in heads:
        s_ref[h] = state[h]


def _gdn_mix(proj, blk, conv_w, alog, dtb, ng, layer, bsz, seq, n_heads):
    hps = GDN_HEADS_PER_STEP
    wblk = hps * GDN_HEAD_DIM
    ngrp = n_heads // hps
    tb = min(GDN_TB, seq)
    nt = seq // tb

    def tok(base):
        return pl.BlockSpec((tb, wblk), lambda b, g, t: (b * nt + t, base + g))

    def cw(base):
        return pl.BlockSpec((None, CONV_WIDTH, wblk), lambda b, g, t: (layer, 0, base + g))

    def vec128():
        return pl.BlockSpec((None, 1, LANES), lambda b, g, t: (layer * ngrp + g, 0, 0))

    return pl.pallas_call(
        _gdn_kernel,
        grid=(bsz, ngrp, nt),
        in_specs=[
            tok(blk['q']), tok(blk['k']), tok(blk['v']), tok(blk['z']),
            pl.BlockSpec((tb, LANES), lambda b, g, t: (b * nt + t, blk['ab'] + g)),
            cw(0), cw(ngrp), cw(2 * ngrp),
            vec128(), vec128(),
            pl.BlockSpec((None, 1, GDN_HEAD_DIM), lambda b, g, t: (layer, 0, 0)),
        ],
        out_specs=pl.BlockSpec((tb, wblk), lambda b, g, t: (b * nt + t, g)),
        out_shape=jax.ShapeDtypeStruct((bsz * seq, n_heads * GDN_HEAD_DIM), F32),
        scratch_shapes=[
            pltpu.VMEM((hps, GDN_HEAD_DIM, GDN_HEAD_DIM), F32),
            pltpu.VMEM((SUBLANES, wblk), F32), pltpu.VMEM((SUBLANES, wblk), F32), pltpu.VMEM((SUBLANES, wblk), F32),
        ],
        compiler_params=_cparams(("parallel", "parallel", "arbitrary")),
        name="gdn",
    )(proj, proj, proj, proj, proj, conv_w, conv_w, conv_w, alog, dtb, ng)


def _out_kernel(alpha, x_ref, ys_ref, yg0_ref, yg1_ref, yl_ref, wglu_ref, bglu_ref, w0_ref, w1_ref, w2_ref, w3_ref,
                g_ref, b_ref, o_ref):
    ys = _gelu_tanh(jnp.concatenate([ys_ref[o] for o in range(ys_ref.shape[0])], axis=1))
    ys = ys * _sigmoid(_mm(ys.astype(BF16), wglu_ref[...]) + bglu_ref[...])
    mixed = _mm(ys.astype(BF16), w0_ref[...])
    mixed += _mm(yg0_ref[...].astype(BF16), w1_ref[...])
    mixed += _mm(yg1_ref[...].astype(BF16), w2_ref[...])
    mixed += _mm(yl_ref[...].astype(BF16), w3_ref[...])
    o_ref[...] = _layer_norm(alpha * x_ref[...] + mixed, g_ref[...], b_ref[...])


def _out_proj_ln(x, y_s5, y_gdn, y_lru, wglu, bglu, w_out, g, b, layer, alpha):
    t, d = x.shape
    noct = y_s5.shape[0]
    wb = noct * y_s5.shape[-1]
    tm = min(OUT_TM, t)

    def act(col):
        return pl.BlockSpec((tm, wb), lambda i: (i, col))

    def wrow(r):
        return pl.BlockSpec((None, wb, d), lambda i: (layer, r, 0))

    def vec(n):
        return pl.BlockSpec((None, 1, n), lambda i: (layer, 0, 0))

    return pl.pallas_call(
        functools.partial(_out_kernel, alpha),
        grid=(t // tm,),
        in_specs=[
            pl.BlockSpec((tm, d), lambda i: (i, 0)),
            pl.BlockSpec((noct, tm, y_s5.shape[-1]), lambda i: (0, i, 0)),
            act(0), act(1), act(0),
            pl.BlockSpec((None, wb, wb), lambda i: (layer, 0, 0)), vec(wb),
            wrow(0), wrow(1), wrow(2), wrow(3),
            vec(d), vec(d),
        ],
        out_specs=pl.BlockSpec((tm, d), lambda i: (i, 0)),
        out_shape=jax.ShapeDtypeStruct((t, d), F32),
        compiler_params=_cparams(("parallel",)),
        name="out_proj_ln",
    )(x, y_s5, y_gdn, y_gdn, y_lru, wglu, bglu, w_out, w_out, w_out, w_out, g, b)


def _row3(p):
    return p.astype(F32).reshape(p.shape[0], 1, p.shape[1])


def kernel(x, ffn1_w_gate, ffn1_w_up, ffn1_w_down, ln1_g, ln1_b, w_in, s5_lambda_re, s5_lambda_im, s5_b_re, s5_b_im,
           s5_c_re, s5_c_im, s5_d, s5_log_step, s5_w_glu, s5_b_glu, gdn_conv_w, gdn_a_log, gdn_dt_bias, gdn_norm_g,
           lru_conv_w, lru_conv_b, lru_w_a, lru_b_a, lru_w_x, lru_b_x, lru_lambda, w_out, ln2_g, ln2_b, ffn2_w_gate,
           ffn2_w_up, ffn2_w_down, ln3_g, ln3_b):
    bsz, seq, d_model = x.shape
    depth = w_in.shape[0]
    t = bsz * seq
    alpha = (2.0 * depth) ** 0.25
    s5_w = s5_d.shape[-1]
    n_heads = gdn_a_log.shape[-1]
    gdn_w = n_heads * GDN_HEAD_DIM
    lru_w = lru_lambda.shape[-1]
    lru_blocks, lru_blk = lru_w_a.shape[1], lru_w_a.shape[2]
    hps = GDN_HEADS_PER_STEP
    ngrp = n_heads // hps
    wblk = hps * GDN_HEAD_DIM
    assert s5_w == PROJ_BLOCK and lru_w == PROJ_BLOCK and gdn_w % wblk == 0 and (4 * gdn_w) % PROJ_BLOCK == 0

    o_qkv = s5_w
    o_a = o_qkv + 4 * gdn_w
    o_b = o_a + n_heads
    o_lx = o_b + n_heads
    main = jnp.concatenate([w_in[..., o_qkv:o_a], w_in[..., o_lx:o_lx + 2 * lru_w]], axis=-1)
    w_u = w_in[..., :s5_w].astype(BF16)
    ab_cols = []
    for g in range(ngrp):
        wa = w_in[..., o_a + g * hps:o_a + (g + 1) * hps]
        wb_ = w_in[..., o_b + g * hps:o_b + (g + 1) * hps]
        ab_cols += [wa, wb_, jnp.zeros(w_in.shape[:2] + (LANES - 2 * hps,), w_in.dtype)]
    n_main = main.shape[-1]
    n_proj = -(-(n_main + ngrp * LANES) // PROJ_TN) * PROJ_TN
    pad = jnp.zeros(w_in.shape[:2] + (n_proj - n_main - ngrp * LANES,), w_in.dtype)
    w_in_p = jnp.concatenate([main] + ab_cols + [pad], axis=-1).astype(BF16)
    blk = {'q': 0, 'k': gdn_w // wblk, 'v': 2 * gdn_w // wblk, 'z': 3 * gdn_w // wblk, 'ab': n_main // LANES}
    lx_blk = (4 * gdn_w) // PROJ_BLOCK
    lg_blk = lx_blk + 1

    def ab_lanes(p):
        pg = p.astype(F32).reshape(depth, ngrp, hps)
        out = jnp.concatenate([pg, pg, jnp.zeros((depth, ngrp, LANES - 2 * hps), F32)], axis=-1)
        return out.reshape(depth * ngrp, 1, LANES)

    alog_l = ab_lanes(gdn_a_log)
    dtb_l = ab_lanes(gdn_dt_bias)

    eye_b = jnp.eye(lru_blocks, dtype=F32)

    def blockdiag(w):
        return jnp.einsum('lhij,hg->lhigj', w.astype(F32), eye_b).reshape(depth, lru_w, lru_w)

    lru_w2 = jnp.concatenate([blockdiag(lru_w_a), blockdiag(lru_w_x)], axis=-1)
    lru_wh, lru_wl = _split2(lru_w2)
    lru_bias = jnp.concatenate([lru_b_a, lru_b_x], axis=-1).astype(F32).reshape(depth, 1, 2 * lru_w)

    bf = lambda w: w.astype(BF16)
    f1gu, f1d = _ffn_gate_up(ffn1_w_gate, ffn1_w_up), bf(ffn1_w_down)
    f2gu, f2d = _ffn_gate_up(ffn2_w_gate, ffn2_w_up), bf(ffn2_w_down)
    w_out_b = bf(w_out)
    wglu_b = bf(s5_w_glu)

    s5_steps = int(math.log2(min(S5_ROWS, seq // S5_CHUNK)))
    s5_mats = jax.vmap(functools.partial(_s5_matrices, nsteps=s5_steps))(
        s5_lambda_re, s5_lambda_im, s5_b_re, s5_b_im, s5_c_re, s5_c_im, s5_d, s5_log_step)

    gdn_conv_f = gdn_conv_w.astype(F32)

    h = x.reshape(t, d_model).astype(F32)
    for l in range(depth):
        h = _ffn_ln(h, f1gu, f1d, _row3(ln1_g), _row3(ln1_b), l, alpha)
        proj, u8 = _in_proj(h, w_in_p, w_u, l)
        y_s5 = _s5_core(u8, s5_mats, l, seq)
        y_gdn = _gdn_mix(proj, blk, gdn_conv_f, alog_l, dtb_l, _row3(gdn_norm_g), l, bsz, seq, n_heads)
        y_lru = _lru_mix(proj, lx_blk, lg_blk, lru_conv_w.astype(F32), _row3(lru_conv_b), lru_wh, lru_wl, lru_bias,
                         _row3(lru_lambda), l, bsz, seq)
        h = _out_proj_ln(h, y_s5, y_gdn, y_lru, wglu_b, _row3(s5_b_glu), w_out_b, _row3(ln2_g), _row3(ln2_b), l,
                         alpha)
        h = _ffn_ln(h, f2gu, f2d, _row3(ln3_g), _row3(ln3_b), l, alpha)
    return h.reshape(bsz, seq, d_model).astype(x.dtype)
```

```python
import functools
import math

import jax
import jax.numpy as jnp
from jax import lax
from jax.experimental import pallas as pl
from jax.experimental.pallas import tpu as pltpu

F32 = jnp.float32
BF16 = jnp.bfloat16

S5_GROUP = 16
S5_STATE = 64
S5_EIG_CLIP = -1e-4
S5_CHUNK = 16
S5_ROWS = 256
S5_PASSES = 1
GDN_HEAD_DIM = 128
GDN_CHUNK = 64
GDN_HEADS_PER_STEP = 8
GDN_SOLVE_PASSES = 1
LRU_C = 8.0
CONV_WIDTH = 4
LN_EPS = 1e-5
RMS_EPS = 1e-6
MACARON_WEIGHT = 0.5

LANES = 128
SUBLANES = 8
VMEM_LIMIT_BYTES = 56 * 1024 * 1024

FFN_TM = 1024
FFN_TF = 256
PROJ_TM = 1024
PROJ_TN = 768
OUT_TM = 512
LRU_TL = 512
GDN_TB = 128
PROJ_BLOCK = 512


def _cparams(sem):
    return pltpu.CompilerParams(dimension_semantics=sem, vmem_limit_bytes=VMEM_LIMIT_BYTES)


def _split2(x):
    hi = x.astype(BF16)
    lo = (x - hi.astype(F32)).astype(BF16)
    return hi, lo


def _mm(a, b):
    return jnp.dot(a, b, preferred_element_type=F32)


def _mm_nt(a, b):
    return lax.dot_general(a, b, (((1,), (1,)), ((), ())), preferred_element_type=F32)


def _mm_tn(a, b):
    return lax.dot_general(a, b, (((0,), (0,)), ((), ())), preferred_element_type=F32)


def _mm3(a, b):
    ah, al = _split2(a)
    bh, bl = _split2(b)
    return _mm(ah, bh) + (_mm(al, bh) + _mm(ah, bl))


def _mm3_nt(a, b):
    ah, al = _split2(a)
    bh, bl = _split2(b)
    return _mm_nt(ah, bh) + (_mm_nt(al, bh) + _mm_nt(ah, bl))


def _mm3_pre(a, bh, bl):
    ah, al = _split2(a)
    return _mm(ah, bh) + (_mm(al, bh) + _mm(ah, bl))


def _layer_norm(y, g, b):
    mu = jnp.mean(y, axis=-1, keepdims=True)
    yc = y - mu
    var = jnp.mean(yc * yc, axis=-1, keepdims=True)
    return yc * lax.rsqrt(var + LN_EPS) * g + b


def _sigmoid(x):
    return 0.5 * jnp.tanh(0.5 * x) + 0.5


def _silu(x):
    return x * _sigmoid(x)


def _gelu_tanh(x):
    c = math.sqrt(2.0 / math.pi)
    return 0.5 * x * (1.0 + jnp.tanh(c * (x + 0.044715 * (x * x * x))))


def _softplus(x):
    return jnp.maximum(x, 0.0) + jnp.log1p(jnp.exp(-jnp.abs(x)))


def _shift_rows(x, s, row):
    return jnp.where(row >= s, pltpu.roll(x, s, 0), 0.0)


def _causal_conv(x, halo, cw):
    return _causal_conv_padded(jnp.concatenate([halo, x], axis=0), SUBLANES, cw)


def _causal_conv_padded(xp, lead, cw):
    y = cw[CONV_WIDTH - 1:CONV_WIDTH] * xp[lead:]
    for s in range(1, CONV_WIDTH):
        k = CONV_WIDTH - 1 - s
        y = y + cw[k:k + 1] * pltpu.roll(xp, s, 0)[lead:]
    return y


def _ffn_kernel(alpha, x_ref, wg_ref, wu_ref, wd_ref, g_ref, b_ref, o_ref, xb_ref):
    j = pl.program_id(1)

    @pl.when(j == 0)
    def _():
        x = x_ref[...]
        xb_ref[...] = x.astype(BF16)
        o_ref[...] = (alpha / MACARON_WEIGHT) * x

    xb = xb_ref[...]
    gate = _mm(xb, wg_ref[...])
    up = _mm(xb, wu_ref[...])
    h = (_silu(gate) * up).astype(BF16)
    o_ref[...] += _mm(h, wd_ref[...])

    @pl.when(j == pl.num_programs(1) - 1)
    def _():
        o_ref[...] = _layer_norm(MACARON_WEIGHT * o_ref[...], g_ref[...], b_ref[...])


def _ffn_ln(x, wg, wu, wd, g, b, layer, alpha):
    t, d = x.shape
    f = wg.shape[-1]
    tm, tf = min(FFN_TM, t), min(FFN_TF, f)
    return pl.pallas_call(
        functools.partial(_ffn_kernel, alpha),
        grid=(t // tm, f // tf),
        in_specs=[
            pl.BlockSpec((tm, d), lambda i, j: (i, 0)),
            pl.BlockSpec((None, d, tf), lambda i, j: (layer, 0, j)),
            pl.BlockSpec((None, d, tf), lambda i, j: (layer, 0, j)),
            pl.BlockSpec((None, tf, d), lambda i, j: (layer, j, 0)),
            pl.BlockSpec((None, 1, d), lambda i, j: (layer, 0, 0)),
            pl.BlockSpec((None, 1, d), lambda i, j: (layer, 0, 0)),
        ],
        out_specs=pl.BlockSpec((tm, d), lambda i, j: (i, 0)),
        out_shape=jax.ShapeDtypeStruct((t, d), F32),
        scratch_shapes=[pltpu.VMEM((tm, d), BF16)],
        compiler_params=_cparams(("parallel", "arbitrary")),
        name="ffn_ln",
    )(x, wg, wu, wd, g, b)


def _proj_kernel(x_ref, w_ref, wu_ref, o_ref, u_ref, xb_ref):
    @pl.when(pl.program_id(1) == 0)
    def _():
        xb = x_ref[...].astype(BF16)
        xb_ref[...] = xb
        u = _mm(xb, wu_ref[...].astype(BF16))
        for o in range(u_ref.shape[0]):
            u_ref[o] = u[:, o * LANES:(o + 1) * LANES]

    o_ref[...] = _mm(xb_ref[...], w_ref[...])


def _in_proj(x, w, w_u, nu, layer):
    t, d = x.shape
    n = w.shape[-1]
    tm, tn = min(PROJ_TM, t), PROJ_TN
    return pl.pallas_call(
        _proj_kernel,
        grid=(t // tm, n // tn),
        in_specs=[
            pl.BlockSpec((tm, d), lambda i, j: (i, 0)),
            pl.BlockSpec((None, d, tn), lambda i, j: (layer, 0, j)),
            pl.BlockSpec((None, d, nu), lambda i, j: (layer, 0, 0)),
        ],
        out_specs=[
            pl.BlockSpec((tm, tn), lambda i, j: (i, j)),
            pl.BlockSpec((nu // LANES, tm, LANES), lambda i, j: (0, i, 0)),
        ],
        out_shape=[jax.ShapeDtypeStruct((t, n), F32), jax.ShapeDtypeStruct((nu // LANES, t, LANES), F32)],
        scratch_shapes=[pltpu.VMEM((tm, d), BF16)],
        compiler_params=_cparams(("parallel", "arbitrary")),
        name="in_proj",
    )(x, w, w_u)


def _s5_expand_operators(k_ref, q_ref, ec_ref, t_ref, g_ref, e_ref):
    cs = k_ref.shape[0]
    n = ec_ref.shape[1] // 2
    gpo = LANES // S5_GROUP
    rgrp = lax.broadcasted_iota(jnp.int32, (LANES, LANES), 0) // S5_GROUP
    lane = lax.broadcasted_iota(jnp.int32, (LANES, LANES), 1)
    lgrp = lane // S5_GROUP
    zero = jnp.zeros((LANES, LANES), BF16)
    bd = [jnp.where(rgrp == lgrp, k_ref[tau], 0.0).astype(BF16) for tau in range(cs)]
    for i in range(cs):
        for j in range(cs):
            t_ref[i * LANES:(i + 1) * LANES, j * LANES:(j + 1) * LANES] = bd[j - i] if j >= i else zero
    lhalf = lane // n
    per_tile = LANES // n
    for i in range(cs):
        q2 = q_ref[i]
        for tile in range(2 * gpo // per_tile):
            c, k = divmod(tile, gpo // per_tile)
            x = q2[:, c * LANES:(c + 1) * LANES]
            g_ref[i * LANES:(i + 1) * LANES, tile * LANES:(tile + 1) * LANES] = jnp.where(
                rgrp == per_tile * k + lhalf, x, 0.0).astype(BF16)
    lgrp_n = lax.broadcasted_iota(jnp.int32, (n, LANES), 1) // S5_GROUP
    for j in range(cs):
        e = ec_ref[j]
        for c in range(2):
            ec = e[c * n:(c + 1) * n]
            for h in range(gpo):
                r0 = (c * gpo + h) * n
                e_ref[r0:r0 + n, j * LANES:(j + 1) * LANES] = jnp.where(lgrp_n == h, ec, 0.0).astype(BF16)


def _s5_kernel(blocks_per_seq, u_ref, k_ref, q_ref, ec_ref, p1_ref, p2_ref, d_ref, o_ref, c_ref, t_ref, g_ref,
               e_ref):
    @pl.when(pl.program_id(1) == 0)
    def _():
        _s5_expand_operators(k_ref, q_ref, ec_ref, t_ref, g_ref, e_ref)

    @pl.when(pl.program_id(1) % blocks_per_seq == 0)
    def _():
        c_ref[...] = jnp.zeros_like(c_ref)

    nrow = u_ref.shape[0] // S5_CHUNK
    u = jnp.concatenate([u_ref[pl.ds(i, nrow, stride=S5_CHUNK), :] for i in range(S5_CHUNK)], axis=1)
    up = _parts(u, S5_PASSES)
    wt, wg, we = t_ref[...], g_ref[...], e_ref[...]
    y = d_ref[...] * u
    z = None
    for a in up:
        y = y + _mm(a, wt)
        z = _mm(a, wg) if z is None else z + _mm(a, wg)
    rows = z.shape[0]
    half = z.shape[1] // 2
    p1 = p1_ref[...]
    p2 = p2_ref[...]
    row = lax.broadcasted_iota(jnp.int32, z.shape, 0)
    carry = c_ref[0:1]
    z = z + jnp.where(row == 0, carry * p1[0:1] + pltpu.roll(carry, half, 1) * p2[0:1], 0.0)
    s, k = 1, 0
    while s < rows:
        zs = _shift_rows(z, s, row)
        z = z + (zs * p1[k:k + 1] + pltpu.roll(zs, half, 1) * p2[k:k + 1])
        s *= 2
        k += 1
    h_in = jnp.where(row == 0, carry, pltpu.roll(z, 1, 0))
    c_ref[0:1] = z[rows - 1:rows]
    for a in _parts(h_in, S5_PASSES):
        y = y + _mm(a, we)
    for i in range(S5_CHUNK):
        o_ref[pl.ds(i, nrow, stride=S5_CHUNK), :] = y[:, i * LANES:(i + 1) * LANES]


def _s5_core(u8, mats, layer, seq):
    kt, q2, ec, p1, p2, d8 = mats
    noct, t, lanes = u8.shape
    width = S5_CHUNK * lanes
    rows = t // S5_CHUNK
    nstate = p1.shape[-1]
    rb = min(S5_ROWS, seq // S5_CHUNK)
    tok = pl.BlockSpec((None, rb * S5_CHUNK, lanes), lambda o, r: (o, r, 0))

    def spec(a):
        return pl.BlockSpec((None, None) + a.shape[2:], lambda o, r: (layer, o) + (0,) * (a.ndim - 2))

    return pl.pallas_call(
        functools.partial(_s5_kernel, (seq // S5_CHUNK) // rb),
        grid=(noct, rows // rb),
        in_specs=[tok, spec(kt), spec(q2), spec(ec), spec(p1), spec(p2), spec(d8)],
        out_specs=tok,
        out_shape=jax.ShapeDtypeStruct((noct, t, lanes), F32),
        scratch_shapes=[pltpu.VMEM((SUBLANES, nstate), F32), pltpu.VMEM((width, width), BF16),
                        pltpu.VMEM((width, nstate), BF16), pltpu.VMEM((nstate, width), BF16)],
        compiler_params=_cparams(("parallel", "arbitrary")),
        name="s5_core",
    )(u8, kt, q2, ec, p1, p2, d8)


def _s5_matrices(lam_re, lam_im, b_re, b_im, c_re, c_im, d, log_step, nsteps):
    groups, n = lam_re.shape
    p = S5_GROUP
    cs = S5_CHUNK
    gpo = LANES // p
    noct = groups // gpo
    hi = lax.Precision.HIGHEST
    lr = jnp.minimum(lam_re.astype(F32), S5_EIG_CLIP)
    li = lam_im.astype(F32)
    dt = jnp.exp(log_step.astype(F32))[:, None]
    zr, zi = lr * dt, li * dt

    def powers(ex):
        mag = jnp.exp(ex[None, :, None] * zr[:, None, :])
        ang = ex[None, :, None] * zi[:, None, :]
        return mag * jnp.cos(ang), mag * jnp.sin(ang)

    pwr, pwi = powers(jnp.arange(cs + 1, dtype=F32))
    ar, ai = pwr[:, 1] - 1.0, pwi[:, 1]
    den = lr * lr + li * li
    cr, ci = (ar * lr + ai * li) / den, (ai * lr - ar * li) / den
    br, bi = b_re.astype(F32), b_im.astype(F32)
    bbr = cr[..., None] * br - ci[..., None] * bi
    bbi = cr[..., None] * bi + ci[..., None] * br
    ccr, cci = c_re.astype(F32), c_im.astype(F32)
    qr = pwr[:, :cs, :, None] * bbr[:, None] - pwi[:, :cs, :, None] * bbi[:, None]
    qi = pwr[:, :cs, :, None] * bbi[:, None] + pwi[:, :cs, :, None] * bbr[:, None]
    kmat = (jnp.einsum('gpn,gtnq->gtpq', ccr, qr, precision=hi)
            - jnp.einsum('gpn,gtnq->gtpq', cci, qi, precision=hi))
    kt = kmat.reshape(noct, gpo, cs, p, p).transpose(0, 2, 1, 4, 3).reshape(noct, cs, gpo * p, p)
    kt = jnp.tile(kt, (1, 1, 1, gpo))
    rev = cs - 1 - jnp.arange(cs)

    def rows_gp(a):
        return a.reshape(noct, gpo, cs, n, p).transpose(0, 2, 1, 4, 3).reshape(noct, cs, gpo * p, n)

    q2r, q2i = rows_gp(qr[:, rev]), rows_gp(qi[:, rev])
    q2 = jnp.concatenate([q2r, q2r, q2i, q2i], axis=-1)
    er = ccr[:, None] * pwr[:, 1:cs + 1, None, :] - cci[:, None] * pwi[:, 1:cs + 1, None, :]
    ei = ccr[:, None] * pwi[:, 1:cs + 1, None, :] + cci[:, None] * pwr[:, 1:cs + 1, None, :]

    def lanes_gp(a):
        return a.reshape(noct, gpo, cs, p, n).transpose(0, 2, 4, 1, 3).reshape(noct, cs, n, gpo * p)

    ec = jnp.concatenate([lanes_gp(er), -lanes_gp(ei)], axis=2)
    mr, mi = powers(cs * (2.0 ** jnp.arange(nsteps, dtype=F32)))
    mr = mr.reshape(noct, gpo, nsteps, n).transpose(0, 2, 1, 3).reshape(noct, nsteps, gpo * n)
    mi = mi.reshape(noct, gpo, nsteps, n).transpose(0, 2, 1, 3).reshape(noct, nsteps, gpo * n)
    pad = ((0, 0), (0, (-nsteps) % SUBLANES), (0, 0))
    p1 = jnp.pad(jnp.concatenate([mr, mr], axis=-1), pad)
    p2 = jnp.pad(jnp.concatenate([-mi, mi], axis=-1), pad)
    d8 = jnp.tile(d.astype(F32).reshape(noct, 1, LANES), (1, 1, cs))
    return kt, q2, ec, p1, p2, d8


def _lru_kernel(x_ref, gate_ref, cw_ref, cb_ref, wh_ref, wl_ref, bias_ref, lam_ref, o_ref, halo_ref, h_ref):
    @pl.when(pl.program_id(1) == 0)
    def _():
        halo_ref[...] = jnp.zeros_like(halo_ref)
        h_ref[...] = jnp.zeros_like(h_ref)

    x = x_ref[...]
    rows, width = x.shape
    xc = _causal_conv(x, halo_ref[...], cw_ref[...]) + cb_ref[...]
    halo_ref[...] = x[rows - SUBLANES:rows]
    gates = _mm3_pre(xc, wh_ref[...], wl_ref[...]) + bias_ref[...]
    r = _sigmoid(gates[:, :width])
    i = _sigmoid(gates[:, width:])
    log_a = (-LRU_C) * r * _softplus(-lam_ref[...])
    a = jnp.exp(log_a)
    b = jnp.sqrt(-jnp.tanh(log_a) * (a * a + 1.0)) * (i * xc)
    srow = lax.broadcasted_iota(jnp.int32, a.shape, 0) & (SUBLANES - 1)
    s = 1
    while s < SUBLANES:
        m = srow >= s
        a_s = jnp.where(m, pltpu.roll(a, s, 0), 1.0)
        b_s = jnp.where(m, pltpu.roll(b, s, 0), 0.0)
        b = a * b_s + b
        a = a * a_s
        s *= 2
    carry = h_ref[0:1]
    tiles = []
    for r in range(rows // SUBLANES):
        sl = slice(r * SUBLANES, (r + 1) * SUBLANES)
        h_tile = a[sl] * carry + b[sl]
        carry = h_tile[SUBLANES - 1:SUBLANES]
        tiles.append(h_tile)
    h_ref[0:1] = carry
    o_ref[...] = jnp.concatenate(tiles, axis=0) * _gelu_tanh(gate_ref[...])


def _lru_mix(proj, x_blk, gate_blk, cw, cb, wh, wl, bias, lam, layer, bsz, seq):
    width = cw.shape[-1]
    tl = min(LRU_TL, seq)
    nt = seq // tl

    def vec(rows):
        return pl.BlockSpec((None, rows, width), lambda b, c: (layer, 0, 0))

    return pl.pallas_call(
        _lru_kernel,
        grid=(bsz, nt),
        in_specs=[
            pl.BlockSpec((tl, width), lambda b, c: (b * nt + c, x_blk)),
            pl.BlockSpec((tl, width), lambda b, c: (b * nt + c, gate_blk)),
            vec(CONV_WIDTH), vec(1),
            pl.BlockSpec((None, width, 2 * width), lambda b, c: (layer, 0, 0)),
            pl.BlockSpec((None, width, 2 * width), lambda b, c: (layer, 0, 0)),
            pl.BlockSpec((None, 1, 2 * width), lambda b, c: (layer, 0, 0)),
            vec(1),
        ],
        out_specs=pl.BlockSpec((tl, width), lambda b, c: (b * nt + c, 0)),
        out_shape=jax.ShapeDtypeStruct((bsz * seq, width), F32),
        scratch_shapes=[pltpu.VMEM((SUBLANES, width), F32), pltpu.VMEM((SUBLANES, width), F32)],
        compiler_params=_cparams(("parallel", "arbitrary")),
        name="rglru",
    )(proj, proj, cw, cb, wh, wl, bias, lam)


def _parts(x, passes):
    if passes == 1:
        return (x.astype(BF16),)
    return _split2(x)


def _mmp(a, b, mm=_mm):
    out = mm(a[0], b[0])
    if len(a) > 1 and len(b) > 1:
        out = out + (mm(a[1], b[0]) + mm(a[0], b[1]))
    return out


def _tri_inverse_all(ms, eye, masks, passes):
    xs = [eye - jnp.where(masks[0], m, 0.0) for m in ms]
    for mk in masks[1:]:
        xp = [_parts(x, passes) for x in xs]
        mp = [_parts(jnp.where(mk, m, 0.0), passes) for m in ms]
        ys = [_mmp(a, b) for a, b in zip(xp, mp)]
        yp = [_parts(y, passes) for y in ys]
        xs = [x - _mmp(a, b) for x, a, b in zip(xs, yp, xp)]
    return xs


def _gdn_kernel(q_ref, k_ref, v_ref, z_ref, ab_ref, cq_ref, ck_ref, cv_ref, alog_ref, dtb_ref, ng_ref, o_ref,
                s_ref, hq_ref, hk_ref, hv_ref):
    @pl.when(pl.program_id(2) == 0)
    def _():
        s_ref[...] = jnp.zeros_like(s_ref)
        hq_ref[...] = jnp.zeros_like(hq_ref)
        hk_ref[...] = jnp.zeros_like(hk_ref)
        hv_ref[...] = jnp.zeros_like(hv_ref)

    rows = q_ref.shape[0]
    cs = GDN_CHUNK
    hd = GDN_HEAD_DIM
    nheads = GDN_HEADS_PER_STEP

    def conv_silu(x_ref, halo_ref, cw_ref):
        x = x_ref[...]
        y = _causal_conv(x, halo_ref[...], cw_ref[...])
        halo_ref[...] = x[rows - SUBLANES:rows]
        return _silu(y)

    q = conv_silu(q_ref, hq_ref, cq_ref)
    k = conv_silu(k_ref, hk_ref, ck_ref)
    v = conv_silu(v_ref, hv_ref, cv_ref)
    z = z_ref[...]

    ab = ab_ref[...]
    g_all = -jnp.exp(alog_ref[...]) * _softplus(ab + dtb_ref[...])
    beta_all = _sigmoid(ab)
    crow = lax.broadcasted_iota(jnp.int32, g_all.shape, 0) & (cs - 1)
    gc = g_all
    s = 1
    while s < cs:
        gc = gc + _shift_rows(gc, s, crow)
        s *= 2
    gc_t = gc.T

    ii = lax.broadcasted_iota(jnp.int32, (cs, cs), 0)
    jj = lax.broadcasted_iota(jnp.int32, (cs, cs), 1)
    lower = ii >= jj
    strict = ii > jj
    eye = jnp.where(ii == jj, 1.0, 0.0).astype(F32)
    masks = [strict & ((ii >> 1) == (jj >> 1))]
    lvl = 1
    while (1 << lvl) < cs:
        masks.append(strict & ((ii >> (lvl + 1)) == (jj >> (lvl + 1))) & ((ii >> lvl) != (jj >> lvl)))
        lvl += 1

    ng = ng_ref[...]
    nchunks = rows // cs
    heads = range(nheads)
    qn, kn = [], []
    for h in heads:
        hs = slice(h * hd, (h + 1) * hd)
        qh, kh = q[:, hs], k[:, hs]
        qn.append(qh * (lax.rsqrt(jnp.sum(qh * qh, axis=-1, keepdims=True) + RMS_EPS) * (hd ** -0.5)))
        kn.append(kh * lax.rsqrt(jnp.sum(kh * kh, axis=-1, keepdims=True) + RMS_EPS))

    probs = [(c, h) for c in range(nchunks) for h in heads]
    rs = lambda c: slice(c * cs, (c + 1) * cs)
    hsl = lambda h: slice(h * hd, (h + 1) * hd)
    qc = [qn[h][rs(c)] for c, h in probs]
    kc = [kn[h][rs(c)] for c, h in probs]
    bc = [jnp.broadcast_to(beta_all[rs(c), nheads + h:nheads + h + 1], (cs, hd)) for c, h in probs]
    gcc = [gc[rs(c), h:h + 1] for c, h in probs]
    dec = [jnp.exp(jnp.where(lower, gcc[p] - gc_t[h:h + 1, rs(c)], -jnp.inf)) for p, (c, h) in enumerate(probs)]
    kb = [a * b for a, b in zip(kc, bc)]
    vb = [v[rs(c), hsl(h)] * bc[p] for p, (c, h) in enumerate(probs)]
    eg = [jnp.broadcast_to(jnp.exp(g), (cs, hd)) for g in gcc]

    kcp = [_parts(a, GDN_SOLVE_PASSES) for a in kc]
    kbp = [_parts(a, GDN_SOLVE_PASSES) for a in kb]
    ms = [jnp.where(strict, _mmp(a, b, _mm_nt) * d, 0.0) for a, b, d in zip(kbp, kcp, dec)]
    tinv = _tri_inverse_all(ms, eye, masks, GDN_SOLVE_PASSES)
    rhs = [jnp.concatenate([a, b * e], axis=1) for a, b, e in zip(vb, kb, eg)]
    sol = [_mmp(_parts(t, GDN_SOLVE_PASSES), _parts(r, GDN_SOLVE_PASSES)) for t, r in zip(tinv, rhs)]
    qcb = [a.astype(BF16) for a in qc]
    qk = [jnp.where(lower, _mm_nt(a, b[0]) * d, 0.0).astype(BF16) for a, b, d in zip(qcb, kcp, dec)]
    qd = [(a * e).astype(BF16) for a, e in zip(qc, eg)]
    glast = [g[cs - 1:cs] for g in gcc]
    kd = [(a * jnp.exp(gl - g)).astype(BF16) for a, gl, g in zip(kc, glast, gcc)]

    state = [s_ref[h] for h in heads]
    for c in range(nchunks):
        ps = [c * nheads + h for h in heads]
        sb = [s.astype(BF16) for s in state]
        v_new = [sol[p][:, :hd] - _mm(sol[p][:, hd:].astype(BF16), sb[h]) for h, p in zip(heads, ps)]
        vnb = [a.astype(BF16) for a in v_new]
        outs = [_mm(qd[p], sb[h]) + _mm(qk[p], vnb[h]) for h, p in zip(heads, ps)]
        state = [state[h] * jnp.exp(glast[p]) + _mm_tn(kd[p], vnb[h]) for h, p in zip(heads, ps)]
        for h, o in zip(heads, outs):
            o = o * lax.rsqrt(jnp.mean(o * o, axis=-1, keepdims=True) + RMS_EPS) * ng
            o_ref[rs(c), hsl(h)] = o * _silu(z[rs(c), hsl(h)])
    for h in heads:
        s_ref[h] = state[h]


def _gdn_mix(proj, blk, conv_w, alog, dtb, ng, layer, bsz, seq, n_heads):
    hps = GDN_HEADS_PER_STEP
    wblk = hps * GDN_HEAD_DIM
    ngrp = n_heads // hps
    tb = min(GDN_TB, seq)
    nt = seq // tb

    def tok(base):
        return pl.BlockSpec((tb, wblk), lambda b, g, t: (b * nt + t, base + g))

    def cw(base):
        return pl.BlockSpec((None, CONV_WIDTH, wblk), lambda b, g, t: (layer, 0, base + g))

    def vec128():
        return pl.BlockSpec((None, 1, LANES), lambda b, g, t: (layer * ngrp + g, 0, 0))

    return pl.pallas_call(
        _gdn_kernel,
        grid=(bsz, ngrp, nt),
        in_specs=[
            tok(blk['q']), tok(blk['k']), tok(blk['v']), tok(blk['z']),
            pl.BlockSpec((tb, LANES), lambda b, g, t: (b * nt + t, blk['ab'] + g)),
            cw(0), cw(ngrp), cw(2 * ngrp),
            vec128(), vec128(),
            pl.BlockSpec((None, 1, GDN_HEAD_DIM), lambda b, g, t: (layer, 0, 0)),
        ],
        out_specs=pl.BlockSpec((tb, wblk), lambda b, g, t: (b * nt + t, g)),
        out_shape=jax.ShapeDtypeStruct((bsz * seq, n_heads * GDN_HEAD_DIM), F32),
        scratch_shapes=[
            pltpu.VMEM((hps, GDN_HEAD_DIM, GDN_HEAD_DIM), F32),
            pltpu.VMEM((SUBLANES, wblk), F32), pltpu.VMEM((SUBLANES, wblk), F32), pltpu.VMEM((SUBLANES, wblk), F32),
        ],
        compiler_params=_cparams(("parallel", "parallel", "arbitrary")),
        name="gdn",
    )(proj, proj, proj, proj, proj, conv_w, conv_w, conv_w, alog, dtb, ng)


def _out_kernel(alpha, x_ref, ys_ref, yg0_ref, yg1_ref, yl_ref, wglu_ref, bglu_ref, w0_ref, w1_ref, w2_ref, w3_ref,
                g_ref, b_ref, o_ref):
    ys = _gelu_tanh(jnp.concatenate([ys_ref[o] for o in range(ys_ref.shape[0])], axis=1))
    ys = ys * _sigmoid(_mm(ys.astype(BF16), wglu_ref[...]) + bglu_ref[...])
    mixed = _mm(ys.astype(BF16), w0_ref[...])
    mixed += _mm(yg0_ref[...].astype(BF16), w1_ref[...])
    mixed += _mm(yg1_ref[...].astype(BF16), w2_ref[...])
    mixed += _mm(yl_ref[...].astype(BF16), w3_ref[...])
    o_ref[...] = _layer_norm(alpha * x_ref[...] + mixed, g_ref[...], b_ref[...])


def _out_proj_ln(x, y_s5, y_gdn, y_lru, wglu, bglu, w_out, g, b, layer, alpha):
    t, d = x.shape
    noct = y_s5.shape[0]
    wb = noct * y_s5.shape[-1]
    tm = min(OUT_TM, t)

    def act(col):
        return pl.BlockSpec((tm, wb), lambda i: (i, col))

    def wrow(r):
        return pl.BlockSpec((None, wb, d), lambda i: (layer, r, 0))

    def vec(n):
        return pl.BlockSpec((None, 1, n), lambda i: (layer, 0, 0))

    return pl.pallas_call(
        functools.partial(_out_kernel, alpha),
        grid=(t // tm,),
        in_specs=[
            pl.BlockSpec((tm, d), lambda i: (i, 0)),
            pl.BlockSpec((noct, tm, y_s5.shape[-1]), lambda i: (0, i, 0)),
            act(0), act(1), act(0),
            pl.BlockSpec((None, wb, wb), lambda i: (layer, 0, 0)), vec(wb),
            wrow(0), wrow(1), wrow(2), wrow(3),
            vec(d), vec(d),
        ],
        out_specs=pl.BlockSpec((tm, d), lambda i: (i, 0)),
        out_shape=jax.ShapeDtypeStruct((t, d), F32),
        compiler_params=_cparams(("parallel",)),
        name="out_proj_ln",
    )(x, y_s5, y_gdn, y_gdn, y_lru, wglu, bglu, w_out, w_out, w_out, w_out, g, b)


def _row3(p):
    return p.astype(F32).reshape(p.shape[0], 1, p.shape[1])


def kernel(x, ffn1_w_gate, ffn1_w_up, ffn1_w_down, ln1_g, ln1_b, w_in, s5_lambda_re, s5_lambda_im, s5_b_re, s5_b_im,
           s5_c_re, s5_c_im, s5_d, s5_log_step, s5_w_glu, s5_b_glu, gdn_conv_w, gdn_a_log, gdn_dt_bias, gdn_norm_g,
           lru_conv_w, lru_conv_b, lru_w_a, lru_b_a, lru_w_x, lru_b_x, lru_lambda, w_out, ln2_g, ln2_b, ffn2_w_gate,
           ffn2_w_up, ffn2_w_down, ln3_g, ln3_b):
    bsz, seq, d_model = x.shape
    depth = w_in.shape[0]
    t = bsz * seq
    alpha = (2.0 * depth) ** 0.25
    s5_w = s5_d.shape[-1]
    n_heads = gdn_a_log.shape[-1]
    gdn_w = n_heads * GDN_HEAD_DIM
    lru_w = lru_lambda.shape[-1]
    lru_blocks, lru_blk = lru_w_a.shape[1], lru_w_a.shape[2]
    hps = GDN_HEADS_PER_STEP
    ngrp = n_heads // hps
    wblk = hps * GDN_HEAD_DIM
    assert s5_w == PROJ_BLOCK and lru_w == PROJ_BLOCK and gdn_w % wblk == 0 and (4 * gdn_w) % PROJ_BLOCK == 0

    o_qkv = s5_w
    o_a = o_qkv + 4 * gdn_w
    o_b = o_a + n_heads
    o_lx = o_b + n_heads
    main = jnp.concatenate([w_in[..., o_qkv:o_a], w_in[..., o_lx:o_lx + 2 * lru_w]], axis=-1)
    w_u = w_in.astype(F32)
    ab_cols = []
    for g in range(ngrp):
        wa = w_in[..., o_a + g * hps:o_a + (g + 1) * hps]
        wb_ = w_in[..., o_b + g * hps:o_b + (g + 1) * hps]
        ab_cols += [wa, wb_, jnp.zeros(w_in.shape[:2] + (LANES - 2 * hps,), w_in.dtype)]
    n_main = main.shape[-1]
    n_proj = -(-(n_main + ngrp * LANES) // PROJ_TN) * PROJ_TN
    pad = jnp.zeros(w_in.shape[:2] + (n_proj - n_main - ngrp * LANES,), w_in.dtype)
    w_in_p = jnp.concatenate([main] + ab_cols + [pad], axis=-1).astype(BF16)
    blk = {'q': 0, 'k': gdn_w // wblk, 'v': 2 * gdn_w // wblk, 'z': 3 * gdn_w // wblk, 'ab': n_main // LANES}
    lx_blk = (4 * gdn_w) // PROJ_BLOCK
    lg_blk = lx_blk + 1

    def ab_lanes(p):
        pg = p.astype(F32).reshape(depth, ngrp, hps)
        out = jnp.concatenate([pg, pg, jnp.zeros((depth, ngrp, LANES - 2 * hps), F32)], axis=-1)
        return out.reshape(depth * ngrp, 1, LANES)

    alog_l = ab_lanes(gdn_a_log)
    dtb_l = ab_lanes(gdn_dt_bias)

    eye_b = jnp.eye(lru_blocks, dtype=F32)

    def blockdiag(w):
        return jnp.einsum('lhij,hg->lhigj', w.astype(F32), eye_b).reshape(depth, lru_w, lru_w)

    lru_w2 = jnp.concatenate([blockdiag(lru_w_a), blockdiag(lru_w_x)], axis=-1)
    lru_wh, lru_wl = _split2(lru_w2)
    lru_bias = jnp.concatenate([lru_b_a, lru_b_x], axis=-1).astype(F32).reshape(depth, 1, 2 * lru_w)

    bf = lambda w: w.astype(BF16)
    f1g, f1u, f1d = bf(ffn1_w_gate), bf(ffn1_w_up), bf(ffn1_w_down)
    f2g, f2u, f2d = bf(ffn2_w_gate), bf(ffn2_w_up), bf(ffn2_w_down)
    w_out_b = bf(w_out)
    wglu_b = bf(s5_w_glu)

    s5_steps = int(math.log2(min(S5_ROWS, seq // S5_CHUNK)))
    s5_mats = jax.vmap(functools.partial(_s5_matrices, nsteps=s5_steps))(
        s5_lambda_re, s5_lambda_im, s5_b_re, s5_b_im, s5_c_re, s5_c_im, s5_d, s5_log_step)

    gdn_conv_f = gdn_conv_w.astype(F32)

    h = x.reshape(t, d_model).astype(F32)
    for l in range(depth):
        h = _ffn_ln(h, f1g, f1u, f1d, _row3(ln1_g), _row3(ln1_b), l, alpha)
        proj, u8 = _in_proj(h, w_in_p, w_u, s5_w, l)
        y_s5 = _s5_core(u8, s5_mats, l, seq)
        y_gdn = _gdn_mix(proj, blk, gdn_conv_f, alog_l, dtb_l, _row3(gdn_norm_g), l, bsz, seq, n_heads)
        y_lru = _lru_mix(proj, lx_blk, lg_blk, lru_conv_w.astype(F32), _row3(lru_conv_b), lru_wh, lru_wl, lru_bias,
                         _row3(lru_lambda), l, bsz, seq)
        h = _out_proj_ln(h, y_s5, y_gdn, y_lru, wglu_b, _row3(s5_b_glu), w_out_b, _row3(ln2_g), _row3(ln2_b), l,
                         alpha)
        h = _ffn_ln(h, f2g, f2u, f2d, _row3(ln3_g), _row3(ln3_b), l, alpha)
    return h.reshape(bsz, seq, d_model).astype(x.dtype)
```

```python
import functools
import math

import jax
import jax.numpy as jnp
from jax import lax
from jax.experimental import pallas as pl
from jax.experimental.pallas import tpu as pltpu

F32 = jnp.float32
BF16 = jnp.bfloat16

S5_GROUP = 16
S5_STATE = 64
S5_EIG_CLIP = -1e-4
S5_CHUNK = 16
S5_ROWS = 256
S5_PASSES = 1
GDN_HEAD_DIM = 128
GDN_CHUNK = 64
GDN_HEADS_PER_STEP = 8
GDN_SOLVE_PASSES = 1
LRU_C = 8.0
CONV_WIDTH = 4
LN_EPS = 1e-5
RMS_EPS = 1e-6
MACARON_WEIGHT = 0.5

LANES = 128
SUBLANES = 8
VMEM_LIMIT_BYTES = 56 * 1024 * 1024

FFN_TM = 1024
FFN_TF = 256
PROJ_TM = 1024
PROJ_TN = 768
OUT_TM = 512
LRU_TL = 512
GDN_TB = 128
PROJ_BLOCK = 512


def _cparams(sem):
    return pltpu.CompilerParams(dimension_semantics=sem, vmem_limit_bytes=VMEM_LIMIT_BYTES)


def _split2(x):
    hi = x.astype(BF16)
    lo = (x - hi.astype(F32)).astype(BF16)
    return hi, lo


def _mm(a, b):
    return jnp.dot(a, b, preferred_element_type=F32)


def _mm_nt(a, b):
    return lax.dot_general(a, b, (((1,), (1,)), ((), ())), preferred_element_type=F32)


def _mm_tn(a, b):
    return lax.dot_general(a, b, (((0,), (0,)), ((), ())), preferred_element_type=F32)


def _mm3(a, b):
    ah, al = _split2(a)
    bh, bl = _split2(b)
    return _mm(ah, bh) + (_mm(al, bh) + _mm(ah, bl))


def _mm3_nt(a, b):
    ah, al = _split2(a)
    bh, bl = _split2(b)
    return _mm_nt(ah, bh) + (_mm_nt(al, bh) + _mm_nt(ah, bl))


def _mm3_pre(a, bh, bl):
    ah, al = _split2(a)
    return _mm(ah, bh) + (_mm(al, bh) + _mm(ah, bl))


def _layer_norm(y, g, b):
    mu = jnp.mean(y, axis=-1, keepdims=True)
    yc = y - mu
    var = jnp.mean(yc * yc, axis=-1, keepdims=True)
    return yc * lax.rsqrt(var + LN_EPS) * g + b


def _sigmoid(x):
    return 0.5 * jnp.tanh(0.5 * x) + 0.5


def _silu(x):
    return x * _sigmoid(x)


def _gelu_tanh(x):
    c = math.sqrt(2.0 / math.pi)
    return 0.5 * x * (1.0 + jnp.tanh(c * (x + 0.044715 * (x * x * x))))


def _softplus(x):
    return jnp.maximum(x, 0.0) + jnp.log1p(jnp.exp(-jnp.abs(x)))


def _shift_rows(x, s, row):
    return jnp.where(row >= s, pltpu.roll(x, s, 0), 0.0)


def _causal_conv(x, halo, cw):
    return _causal_conv_padded(jnp.concatenate([halo, x], axis=0), SUBLANES, cw)


def _causal_conv_padded(xp, lead, cw):
    y = cw[CONV_WIDTH - 1:CONV_WIDTH] * xp[lead:]
    for s in range(1, CONV_WIDTH):
        k = CONV_WIDTH - 1 - s
        y = y + cw[k:k + 1] * pltpu.roll(xp, s, 0)[lead:]
    return y


def _ffn_kernel(alpha, x_ref, wg_ref, wu_ref, wd_ref, g_ref, b_ref, o_ref, xb_ref):
    j = pl.program_id(1)

    @pl.when(j == 0)
    def _():
        x = x_ref[...]
        xb_ref[...] = x.astype(BF16)
        o_ref[...] = (alpha / MACARON_WEIGHT) * x

    xb = xb_ref[...]
    gate = _mm(xb, wg_ref[...])
    up = _mm(xb, wu_ref[...])
    h = (_silu(gate) * up).astype(BF16)
    o_ref[...] += _mm(h, wd_ref[...])

    @pl.when(j == pl.num_programs(1) - 1)
    def _():
        o_ref[...] = _layer_norm(MACARON_WEIGHT * o_ref[...], g_ref[...], b_ref[...])


def _ffn_ln(x, wg, wu, wd, g, b, layer, alpha):
    t, d = x.shape
    f = wg.shape[-1]
    tm, tf = min(FFN_TM, t), min(FFN_TF, f)
    return pl.pallas_call(
        functools.partial(_ffn_kernel, alpha),
        grid=(t // tm, f // tf),
        in_specs=[
            pl.BlockSpec((tm, d), lambda i, j: (i, 0)),
            pl.BlockSpec((None, d, tf), lambda i, j: (layer, 0, j)),
            pl.BlockSpec((None, d, tf), lambda i, j: (layer, 0, j)),
            pl.BlockSpec((None, tf, d), lambda i, j: (layer, j, 0)),
            pl.BlockSpec((None, 1, d), lambda i, j: (layer, 0, 0)),
            pl.BlockSpec((None, 1, d), lambda i, j: (layer, 0, 0)),
        ],
        out_specs=pl.BlockSpec((tm, d), lambda i, j: (i, 0)),
        out_shape=jax.ShapeDtypeStruct((t, d), F32),
        scratch_shapes=[pltpu.VMEM((tm, d), BF16)],
        compiler_params=_cparams(("parallel", "arbitrary")),
        name="ffn_ln",
    )(x, wg, wu, wd, g, b)


def _proj_kernel(x_ref, w_ref, wu_ref, o_ref, u_ref, xb_ref):
    @pl.when(pl.program_id(1) == 0)
    def _():
        xb = x_ref[...].astype(BF16)
        xb_ref[...] = xb
        u = _mm(xb, wu_ref[...].astype(BF16))
        for o in range(u_ref.shape[0]):
            u_ref[o] = u[:, o * LANES:(o + 1) * LANES]

    o_ref[...] = _mm(xb_ref[...], w_ref[...])


def _in_proj(x, w, n, nu, u_blk, layer):
    t, d = x.shape
    w_u = w
    tm, tn = min(PROJ_TM, t), PROJ_TN
    return pl.pallas_call(
        _proj_kernel,
        grid=(t // tm, n // tn),
        in_specs=[
            pl.BlockSpec((tm, d), lambda i, j: (i, 0)),
            pl.BlockSpec((None, d, tn), lambda i, j: (layer, 0, j)),
            pl.BlockSpec((None, d, nu), lambda i, j: (layer, 0, u_blk)),
        ],
        out_specs=[
            pl.BlockSpec((tm, tn), lambda i, j: (i, j)),
            pl.BlockSpec((nu // LANES, tm, LANES), lambda i, j: (0, i, 0)),
        ],
        out_shape=[jax.ShapeDtypeStruct((t, n), F32), jax.ShapeDtypeStruct((nu // LANES, t, LANES), F32)],
        scratch_shapes=[pltpu.VMEM((tm, d), BF16)],
        compiler_params=_cparams(("parallel", "arbitrary")),
        name="in_proj",
    )(x, w, w_u)


def _s5_expand_operators(k_ref, q_ref, ec_ref, t_ref, g_ref, e_ref):
    cs = k_ref.shape[0]
    n = ec_ref.shape[1] // 2
    gpo = LANES // S5_GROUP
    rgrp = lax.broadcasted_iota(jnp.int32, (LANES, LANES), 0) // S5_GROUP
    lane = lax.broadcasted_iota(jnp.int32, (LANES, LANES), 1)
    lgrp = lane // S5_GROUP
    zero = jnp.zeros((LANES, LANES), BF16)
    bd = [jnp.where(rgrp == lgrp, k_ref[tau], 0.0).astype(BF16) for tau in range(cs)]
    for i in range(cs):
        for j in range(cs):
            t_ref[i * LANES:(i + 1) * LANES, j * LANES:(j + 1) * LANES] = bd[j - i] if j >= i else zero
    lhalf = lane // n
    per_tile = LANES // n
    for i in range(cs):
        q2 = q_ref[i]
        for tile in range(2 * gpo // per_tile):
            c, k = divmod(tile, gpo // per_tile)
            x = q2[:, c * LANES:(c + 1) * LANES]
            g_ref[i * LANES:(i + 1) * LANES, tile * LANES:(tile + 1) * LANES] = jnp.where(
                rgrp == per_tile * k + lhalf, x, 0.0).astype(BF16)
    lgrp_n = lax.broadcasted_iota(jnp.int32, (n, LANES), 1) // S5_GROUP
    for j in range(cs):
        e = ec_ref[j]
        for c in range(2):
            ec = e[c * n:(c + 1) * n]
            for h in range(gpo):
                r0 = (c * gpo + h) * n
                e_ref[r0:r0 + n, j * LANES:(j + 1) * LANES] = jnp.where(lgrp_n == h, ec, 0.0).astype(BF16)


def _s5_kernel(blocks_per_seq, u_ref, k_ref, q_ref, ec_ref, p1_ref, p2_ref, d_ref, o_ref, c_ref, t_ref, g_ref,
               e_ref):
    @pl.when(pl.program_id(1) == 0)
    def _():
        _s5_expand_operators(k_ref, q_ref, ec_ref, t_ref, g_ref, e_ref)

    @pl.when(pl.program_id(1) % blocks_per_seq == 0)
    def _():
        c_ref[...] = jnp.zeros_like(c_ref)

    nrow = u_ref.shape[0] // S5_CHUNK
    u = jnp.concatenate([u_ref[pl.ds(i, nrow, stride=S5_CHUNK), :] for i in range(S5_CHUNK)], axis=1)
    up = _parts(u, S5_PASSES)
    wt, wg, we = t_ref[...], g_ref[...], e_ref[...]
    y = d_ref[...] * u
    z = None
    for a in up:
        y = y + _mm(a, wt)
        z = _mm(a, wg) if z is None else z + _mm(a, wg)
    rows = z.shape[0]
    half = z.shape[1] // 2
    p1 = p1_ref[...]
    p2 = p2_ref[...]
    row = lax.broadcasted_iota(jnp.int32, z.shape, 0)
    carry = c_ref[0:1]
    z = z + jnp.where(row == 0, carry * p1[0:1] + pltpu.roll(carry, half, 1) * p2[0:1], 0.0)
    s, k = 1, 0
    while s < rows:
        zs = _shift_rows(z, s, row)
        z = z + (zs * p1[k:k + 1] + pltpu.roll(zs, half, 1) * p2[k:k + 1])
        s *= 2
        k += 1
    h_in = jnp.where(row == 0, carry, pltpu.roll(z, 1, 0))
    c_ref[0:1] = z[rows - 1:rows]
    for a in _parts(h_in, S5_PASSES):
        y = y + _mm(a, we)
    for i in range(S5_CHUNK):
        o_ref[pl.ds(i, nrow, stride=S5_CHUNK), :] = y[:, i * LANES:(i + 1) * LANES]


def _s5_core(u8, mats, layer, seq):
    kt, q2, ec, p1, p2, d8 = mats
    noct, t, lanes = u8.shape
    width = S5_CHUNK * lanes
    rows = t // S5_CHUNK
    nstate = p1.shape[-1]
    rb = min(S5_ROWS, seq // S5_CHUNK)
    tok = pl.BlockSpec((None, rb * S5_CHUNK, lanes), lambda o, r: (o, r, 0))

    def spec(a):
        return pl.BlockSpec((None, None) + a.shape[2:], lambda o, r: (layer, o) + (0,) * (a.ndim - 2))

    return pl.pallas_call(
        functools.partial(_s5_kernel, (seq // S5_CHUNK) // rb),
        grid=(noct, rows // rb),
        in_specs=[tok, spec(kt), spec(q2), spec(ec), spec(p1), spec(p2), spec(d8)],
        out_specs=tok,
        out_shape=jax.ShapeDtypeStruct((noct, t, lanes), F32),
        scratch_shapes=[pltpu.VMEM((SUBLANES, nstate), F32), pltpu.VMEM((width, width), BF16),
                        pltpu.VMEM((width, nstate), BF16), pltpu.VMEM((nstate, width), BF16)],
        compiler_params=_cparams(("parallel", "arbitrary")),
        name="s5_core",
    )(u8, kt, q2, ec, p1, p2, d8)


def _s5_matrices(lam_re, lam_im, b_re, b_im, c_re, c_im, d, log_step, nsteps):
    groups, n = lam_re.shape
    p = S5_GROUP
    cs = S5_CHUNK
    gpo = LANES // p
    noct = groups // gpo
    hi = lax.Precision.HIGHEST
    lr = jnp.minimum(lam_re.astype(F32), S5_EIG_CLIP)
    li = lam_im.astype(F32)
    dt = jnp.exp(log_step.astype(F32))[:, None]
    zr, zi = lr * dt, li * dt

    def powers(ex):
        mag = jnp.exp(ex[None, :, None] * zr[:, None, :])
        ang = ex[None, :, None] * zi[:, None, :]
        return mag * jnp.cos(ang), mag * jnp.sin(ang)

    pwr, pwi = powers(jnp.arange(cs + 1, dtype=F32))
    ar, ai = pwr[:, 1] - 1.0, pwi[:, 1]
    den = lr * lr + li * li
    cr, ci = (ar * lr + ai * li) / den, (ai * lr - ar * li) / den
    br, bi = b_re.astype(F32), b_im.astype(F32)
    bbr = cr[..., None] * br - ci[..., None] * bi
    bbi = cr[..., None] * bi + ci[..., None] * br
    ccr, cci = c_re.astype(F32), c_im.astype(F32)
    qr = pwr[:, :cs, :, None] * bbr[:, None] - pwi[:, :cs, :, None] * bbi[:, None]
    qi = pwr[:, :cs, :, None] * bbi[:, None] + pwi[:, :cs, :, None] * bbr[:, None]
    kmat = (jnp.einsum('gpn,gtnq->gtpq', ccr, qr, precision=hi)
            - jnp.einsum('gpn,gtnq->gtpq', cci, qi, precision=hi))
    kt = kmat.reshape(noct, gpo, cs, p, p).transpose(0, 2, 1, 4, 3).reshape(noct, cs, gpo * p, p)
    kt = jnp.tile(kt, (1, 1, 1, gpo))
    rev = cs - 1 - jnp.arange(cs)

    def rows_gp(a):
        return a.reshape(noct, gpo, cs, n, p).transpose(0, 2, 1, 4, 3).reshape(noct, cs, gpo * p, n)

    q2r, q2i = rows_gp(qr[:, rev]), rows_gp(qi[:, rev])
    q2 = jnp.concatenate([q2r, q2r, q2i, q2i], axis=-1)
    er = ccr[:, None] * pwr[:, 1:cs + 1, None, :] - cci[:, None] * pwi[:, 1:cs + 1, None, :]
    ei = ccr[:, None] * pwi[:, 1:cs + 1, None, :] + cci[:, None] * pwr[:, 1:cs + 1, None, :]

    def lanes_gp(a):
        return a.reshape(noct, gpo, cs, p, n).transpose(0, 2, 4, 1, 3).reshape(noct, cs, n, gpo * p)

    ec = jnp.concatenate([lanes_gp(er), -lanes_gp(ei)], axis=2)
    mr, mi = powers(cs * (2.0 ** jnp.arange(nsteps, dtype=F32)))
    mr = mr.reshape(noct, gpo, nsteps, n).transpose(0, 2, 1, 3).reshape(noct, nsteps, gpo * n)
    mi = mi.reshape(noct, gpo, nsteps, n).transpose(0, 2, 1, 3).reshape(noct, nsteps, gpo * n)
    pad = ((0, 0), (0, (-nsteps) % SUBLANES), (0, 0))
    p1 = jnp.pad(jnp.concatenate([mr, mr], axis=-1), pad)
    p2 = jnp.pad(jnp.concatenate([-mi, mi], axis=-1), pad)
    d8 = jnp.tile(d.astype(F32).reshape(noct, 1, LANES), (1, 1, cs))
    return kt, q2, ec, p1, p2, d8


def _lru_kernel(x_ref, gate_ref, cw_ref, cb_ref, wh_ref, wl_ref, bias_ref, lam_ref, o_ref, halo_ref, h_ref):
    @pl.when(pl.program_id(1) == 0)
    def _():
        halo_ref[...] = jnp.zeros_like(halo_ref)
        h_ref[...] = jnp.zeros_like(h_ref)

    x = x_ref[...]
    rows, width = x.shape
    xc = _causal_conv(x, halo_ref[...], cw_ref[...]) + cb_ref[...]
    halo_ref[...] = x[rows - SUBLANES:rows]
    gates = _mm3_pre(xc, wh_ref[...], wl_ref[...]) + bias_ref[...]
    r = _sigmoid(gates[:, :width])
    i = _sigmoid(gates[:, width:])
    log_a = (-LRU_C) * r * _softplus(-lam_ref[...])
    a = jnp.exp(log_a)
    b = jnp.sqrt(-jnp.tanh(log_a) * (a * a + 1.0)) * (i * xc)
    srow = lax.broadcasted_iota(jnp.int32, a.shape, 0) & (SUBLANES - 1)
    s = 1
    while s < SUBLANES:
        m = srow >= s
        a_s = jnp.where(m, pltpu.roll(a, s, 0), 1.0)
        b_s = jnp.where(m, pltpu.roll(b, s, 0), 0.0)
        b = a * b_s + b
        a = a * a_s
        s *= 2
    carry = h_ref[0:1]
    tiles = []
    for r in range(rows // SUBLANES):
        sl = slice(r * SUBLANES, (r + 1) * SUBLANES)
        h_tile = a[sl] * carry + b[sl]
        carry = h_tile[SUBLANES - 1:SUBLANES]
        tiles.append(h_tile)
    h_ref[0:1] = carry
    o_ref[...] = jnp.concatenate(tiles, axis=0) * _gelu_tanh(gate_ref[...])


def _lru_mix(proj, x_blk, gate_blk, cw, cb, wh, wl, bias, lam, layer, bsz, seq):
    width = cw.shape[-1]
    tl = min(LRU_TL, seq)
    nt = seq // tl

    def vec(rows):
        return pl.BlockSpec((None, rows, width), lambda b, c: (layer, 0, 0))

    return pl.pallas_call(
        _lru_kernel,
        grid=(bsz, nt),
        in_specs=[
            pl.BlockSpec((tl, width), lambda b, c: (b * nt + c, x_blk)),
            pl.BlockSpec((tl, width), lambda b, c: (b * nt + c, gate_blk)),
            vec(CONV_WIDTH), vec(1),
            pl.BlockSpec((None, width, 2 * width), lambda b, c: (layer, 0, 0)),
            pl.BlockSpec((None, width, 2 * width), lambda b, c: (layer, 0, 0)),
            pl.BlockSpec((None, 1, 2 * width), lambda b, c: (layer, 0, 0)),
            vec(1),
        ],
        out_specs=pl.BlockSpec((tl, width), lambda b, c: (b * nt + c, 0)),
        out_shape=jax.ShapeDtypeStruct((bsz * seq, width), F32),
        scratch_shapes=[pltpu.VMEM((SUBLANES, width), F32), pltpu.VMEM((SUBLANES, width), F32)],
        compiler_params=_cparams(("parallel", "arbitrary")),
        name="rglru",
    )(proj, proj, cw, cb, wh, wl, bias, lam)


def _parts(x, passes):
    if passes == 1:
        return (x.astype(BF16),)
    return _split2(x)


def _mmp(a, b, mm=_mm):
    out = mm(a[0], b[0])
    if len(a) > 1 and len(b) > 1:
        out = out + (mm(a[1], b[0]) + mm(a[0], b[1]))
    return out


def _tri_inverse_all(ms, eye, masks, passes):
    xs = [eye - jnp.where(masks[0], m, 0.0) for m in ms]
    for mk in masks[1:]:
        xp = [_parts(x, passes) for x in xs]
        mp = [_parts(jnp.where(mk, m, 0.0), passes) for m in ms]
        ys = [_mmp(a, b) for a, b in zip(xp, mp)]
        yp = [_parts(y, passes) for y in ys]
        xs = [x - _mmp(a, b) for x, a, b in zip(xs, yp, xp)]
    return xs


def _gdn_kernel(q_ref, k_ref, v_ref, z_ref, ab_ref, cq_ref, ck_ref, cv_ref, alog_ref, dtb_ref, ng_ref, o_ref,
                s_ref, hq_ref, hk_ref, hv_ref):
    @pl.when(pl.program_id(2) == 0)
    def _():
        s_ref[...] = jnp.zeros_like(s_ref)
        hq_ref[...] = jnp.zeros_like(hq_ref)
        hk_ref[...] = jnp.zeros_like(hk_ref)
        hv_ref[...] = jnp.zeros_like(hv_ref)

    rows = q_ref.shape[0]
    cs = GDN_CHUNK
    hd = GDN_HEAD_DIM
    nheads = GDN_HEADS_PER_STEP

    def conv_silu(x_ref, halo_ref, cw_ref):
        x = x_ref[...]
        y = _causal_conv(x, halo_ref[...], cw_ref[...])
        halo_ref[...] = x[rows - SUBLANES:rows]
        return _silu(y)

    q = conv_silu(q_ref, hq_ref, cq_ref)
    k = conv_silu(k_ref, hk_ref, ck_ref)
    v = conv_silu(v_ref, hv_ref, cv_ref)
    z = z_ref[...]

    ab = ab_ref[...]
    g_all = -jnp.exp(alog_ref[...]) * _softplus(ab + dtb_ref[...])
    beta_all = _sigmoid(ab)
    crow = lax.broadcasted_iota(jnp.int32, g_all.shape, 0) & (cs - 1)
    gc = g_all
    s = 1
    while s < cs:
        gc = gc + _shift_rows(gc, s, crow)
        s *= 2
    gc_t = gc.T

    ii = lax.broadcasted_iota(jnp.int32, (cs, cs), 0)
    jj = lax.broadcasted_iota(jnp.int32, (cs, cs), 1)
    lower = ii >= jj
    strict = ii > jj
    eye = jnp.where(ii == jj, 1.0, 0.0).astype(F32)
    masks = [strict & ((ii >> 1) == (jj >> 1))]
    lvl = 1
    while (1 << lvl) < cs:
        masks.append(strict & ((ii >> (lvl + 1)) == (jj >> (lvl + 1))) & ((ii >> lvl) != (jj >> lvl)))
        lvl += 1

    ng = ng_ref[...]
    nchunks = rows // cs
    heads = range(nheads)
    qn, kn = [], []
    for h in heads:
        hs = slice(h * hd, (h + 1) * hd)
        qh, kh = q[:, hs], k[:, hs]
        qn.append(qh * (lax.rsqrt(jnp.sum(qh * qh, axis=-1, keepdims=True) + RMS_EPS) * (hd ** -0.5)))
        kn.append(kh * lax.rsqrt(jnp.sum(kh * kh, axis=-1, keepdims=True) + RMS_EPS))

    probs = [(c, h) for c in range(nchunks) for h in heads]
    rs = lambda c: slice(c * cs, (c + 1) * cs)
    hsl = lambda h: slice(h * hd, (h + 1) * hd)
    qc = [qn[h][rs(c)] for c, h in probs]
    kc = [kn[h][rs(c)] for c, h in probs]
    bc = [jnp.broadcast_to(beta_all[rs(c), nheads + h:nheads + h + 1], (cs, hd)) for c, h in probs]
    gcc = [gc[rs(c), h:h + 1] for c, h in probs]
    dec = [jnp.exp(jnp.where(lower, gcc[p] - gc_t[h:h + 1, rs(c)], -jnp.inf)) for p, (c, h) in enumerate(probs)]
    kb = [a * b for a, b in zip(kc, bc)]
    vb = [v[rs(c), hsl(h)] * bc[p] for p, (c, h) in enumerate(probs)]
    eg = [jnp.broadcast_to(jnp.exp(g), (cs, hd)) for g in gcc]

    kcp = [_parts(a, GDN_SOLVE_PASSES) for a in kc]
    kbp = [_parts(a, GDN_SOLVE_PASSES) for a in kb]
    ms = [jnp.where(strict, _mmp(a, b, _mm_nt) * d, 0.0) for a, b, d in zip(kbp, kcp, dec)]
    tinv = _tri_inverse_all(ms, eye, masks, GDN_SOLVE_PASSES)
    rhs = [jnp.concatenate([a, b * e], axis=1) for a, b, e in zip(vb, kb, eg)]
    sol = [_mmp(_parts(t, GDN_SOLVE_PASSES), _parts(r, GDN_SOLVE_PASSES)) for t, r in zip(tinv, rhs)]
    qcb = [a.astype(BF16) for a in qc]
    qk = [jnp.where(lower, _mm_nt(a, b[0]) * d, 0.0).astype(BF16) for a, b, d in zip(qcb, kcp, dec)]
    qd = [(a * e).astype(BF16) for a, e in zip(qc, eg)]
    glast = [g[cs - 1:cs] for g in gcc]
    kd = [(a * jnp.exp(gl - g)).astype(BF16) for a, gl, g in zip(kc, glast, gcc)]

    state = [s_ref[h] for h in heads]
    for c in range(nchunks):
        ps = [c * nheads + h for h in heads]
        sb = [s.astype(BF16) for s in state]
        v_new = [sol[p][:, :hd] - _mm(sol[p][:, hd:].astype(BF16), sb[h]) for h, p in zip(heads, ps)]
        vnb = [a.astype(BF16) for a in v_new]
        outs = [_mm(qd[p], sb[h]) + _mm(qk[p], vnb[h]) for h, p in zip(heads, ps)]
        state = [state[h] * jnp.exp(glast[p]) + _mm_tn(kd[p], vnb[h]) for h, p in zip(heads, ps)]
        for h, o in zip(heads, outs):
            o = o * lax.rsqrt(jnp.mean(o * o, axis=-1, keepdims=True) + RMS_EPS) * ng
            o_ref[rs(c), hsl(h)] = o * _silu(z[rs(c), hsl(h)])
    for h in heads:
        s_ref[h] = state[h]


def _gdn_mix(proj, blk, conv_w, alog, dtb, ng, layer, bsz, seq, n_heads):
    hps = GDN_HEADS_PER_STEP
    wblk = hps * GDN_HEAD_DIM
    ngrp = n_heads // hps
    tb = min(GDN_TB, seq)
    nt = seq // tb

    def tok(base):
        return pl.BlockSpec((tb, wblk), lambda b, g, t: (b * nt + t, base + g))

    def cw(base):
        return pl.BlockSpec((None, CONV_WIDTH, wblk), lambda b, g, t: (layer, 0, base + g))

    def vec128():
        return pl.BlockSpec((None, 1, LANES), lambda b, g, t: (layer * ngrp + g, 0, 0))

    return pl.pallas_call(
        _gdn_kernel,
        grid=(bsz, ngrp, nt),
        in_specs=[
            tok(blk['q']), tok(blk['k']), tok(blk['v']), tok(blk['z']),
            pl.BlockSpec((tb, LANES), lambda b, g, t: (b * nt + t, blk['ab'] + g)),
            cw(0), cw(ngrp), cw(2 * ngrp),
            vec128(), vec128(),
            pl.BlockSpec((None, 1, GDN_HEAD_DIM), lambda b, g, t: (layer, 0, 0)),
        ],
        out_specs=pl.BlockSpec((tb, wblk), lambda b, g, t: (b * nt + t, g)),
        out_shape=jax.ShapeDtypeStruct((bsz * seq, n_heads * GDN_HEAD_DIM), F32),
        scratch_shapes=[
            pltpu.VMEM((hps, GDN_HEAD_DIM, GDN_HEAD_DIM), F32),
            pltpu.VMEM((SUBLANES, wblk), F32), pltpu.VMEM((SUBLANES, wblk), F32), pltpu.VMEM((SUBLANES, wblk), F32),
        ],
        compiler_params=_cparams(("parallel", "parallel", "arbitrary")),
        name="gdn",
    )(proj, proj, proj, proj, proj, conv_w, conv_w, conv_w, alog, dtb, ng)


def _out_kernel(alpha, x_ref, ys_ref, yg0_ref, yg1_ref, yl_ref, wglu_ref, bglu_ref, w0_ref, w1_ref, w2_ref, w3_ref,
                g_ref, b_ref, o_ref):
    ys = _gelu_tanh(jnp.concatenate([ys_ref[o] for o in range(ys_ref.shape[0])], axis=1))
    ys = ys * _sigmoid(_mm(ys.astype(BF16), wglu_ref[...]) + bglu_ref[...])
    mixed = _mm(ys.astype(BF16), w0_ref[...])
    mixed += _mm(yg0_ref[...].astype(BF16), w1_ref[...])
    mixed += _mm(yg1_ref[...].astype(BF16), w2_ref[...])
    mixed += _mm(yl_ref[...].astype(BF16), w3_ref[...])
    o_ref[...] = _layer_norm(alpha * x_ref[...] + mixed, g_ref[...], b_ref[...])


def _out_proj_ln(x, y_s5, y_gdn, y_lru, wglu, bglu, w_out, g, b, layer, alpha):
    t, d = x.shape
    noct = y_s5.shape[0]
    wb = noct * y_s5.shape[-1]
    tm = min(OUT_TM, t)

    def act(col):
        return pl.BlockSpec((tm, wb), lambda i: (i, col))

    def wrow(r):
        return pl.BlockSpec((None, wb, d), lambda i: (layer, r, 0))

    def vec(n):
        return pl.BlockSpec((None, 1, n), lambda i: (layer, 0, 0))

    return pl.pallas_call(
        functools.partial(_out_kernel, alpha),
        grid=(t // tm,),
        in_specs=[
            pl.BlockSpec((tm, d), lambda i: (i, 0)),
            pl.BlockSpec((noct, tm, y_s5.shape[-1]), lambda i: (0, i, 0)),
            act(0), act(1), act(0),
            pl.BlockSpec((None, wb, wb), lambda i: (layer, 0, 0)), vec(wb),
            wrow(0), wrow(1), wrow(2), wrow(3),
            vec(d), vec(d),
        ],
        out_specs=pl.BlockSpec((tm, d), lambda i: (i, 0)),
        out_shape=jax.ShapeDtypeStruct((t, d), F32),
        compiler_params=_cparams(("parallel",)),
        name="out_proj_ln",
    )(x, y_s5, y_gdn, y_gdn, y_lru, wglu, bglu, w_out, w_out, w_out, w_out, g, b)


def _row3(p):
    return p.astype(F32).reshape(p.shape[0], 1, p.shape[1])


def kernel(x, ffn1_w_gate, ffn1_w_up, ffn1_w_down, ln1_g, ln1_b, w_in, s5_lambda_re, s5_lambda_im, s5_b_re, s5_b_im,
           s5_c_re, s5_c_im, s5_d, s5_log_step, s5_w_glu, s5_b_glu, gdn_conv_w, gdn_a_log, gdn_dt_bias, gdn_norm_g,
           lru_conv_w, lru_conv_b, lru_w_a, lru_b_a, lru_w_x, lru_b_x, lru_lambda, w_out, ln2_g, ln2_b, ffn2_w_gate,
           ffn2_w_up, ffn2_w_down, ln3_g, ln3_b):
    bsz, seq, d_model = x.shape
    depth = w_in.shape[0]
    t = bsz * seq
    alpha = (2.0 * depth) ** 0.25
    s5_w = s5_d.shape[-1]
    n_heads = gdn_a_log.shape[-1]
    gdn_w = n_heads * GDN_HEAD_DIM
    lru_w = lru_lambda.shape[-1]
    lru_blocks, lru_blk = lru_w_a.shape[1], lru_w_a.shape[2]
    hps = GDN_HEADS_PER_STEP
    ngrp = n_heads // hps
    wblk = hps * GDN_HEAD_DIM
    assert s5_w == PROJ_BLOCK and lru_w == PROJ_BLOCK and gdn_w % wblk == 0 and (4 * gdn_w) % PROJ_BLOCK == 0

    o_qkv = s5_w
    o_a = o_qkv + 4 * gdn_w
    o_b = o_a + n_heads
    o_lx = o_b + n_heads
    main = jnp.concatenate([w_in[..., o_qkv:o_a], w_in[..., o_lx:o_lx + 2 * lru_w]], axis=-1)
    ab_cols = []
    for g in range(ngrp):
        wa = w_in[..., o_a + g * hps:o_a + (g + 1) * hps]
        wb_ = w_in[..., o_b + g * hps:o_b + (g + 1) * hps]
        ab_cols += [wa, wb_, jnp.zeros(w_in.shape[:2] + (LANES - 2 * hps,), w_in.dtype)]
    n_main = main.shape[-1]
    n_proj = -(-(n_main + ngrp * LANES) // PROJ_TN) * PROJ_TN
    u_blk = -(-n_proj // s5_w)
    pad = jnp.zeros(w_in.shape[:2] + (u_blk * s5_w - n_main - ngrp * LANES,), w_in.dtype)
    w_in_p = jnp.concatenate([main] + ab_cols + [pad, w_in[..., :s5_w]], axis=-1).astype(BF16)
    blk = {'q': 0, 'k': gdn_w // wblk, 'v': 2 * gdn_w // wblk, 'z': 3 * gdn_w // wblk, 'ab': n_main // LANES}
    lx_blk = (4 * gdn_w) // PROJ_BLOCK
    lg_blk = lx_blk + 1

    def ab_lanes(p):
        pg = p.astype(F32).reshape(depth, ngrp, hps)
        out = jnp.concatenate([pg, pg, jnp.zeros((depth, ngrp, LANES - 2 * hps), F32)], axis=-1)
        return out.reshape(depth * ngrp, 1, LANES)

    alog_l = ab_lanes(gdn_a_log)
    dtb_l = ab_lanes(gdn_dt_bias)

    eye_b = jnp.eye(lru_blocks, dtype=F32)

    def blockdiag(w):
        return jnp.einsum('lhij,hg->lhigj', w.astype(F32), eye_b).reshape(depth, lru_w, lru_w)

    lru_w2 = jnp.concatenate([blockdiag(lru_w_a), blockdiag(lru_w_x)], axis=-1)
    lru_wh, lru_wl = _split2(lru_w2)
    lru_bias = jnp.concatenate([lru_b_a, lru_b_x], axis=-1).astype(F32).reshape(depth, 1, 2 * lru_w)

    bf = lambda w: w.astype(BF16)
    f1g, f1u, f1d = bf(ffn1_w_gate), bf(ffn1_w_up), bf(ffn1_w_down)
    f2g, f2u, f2d = bf(ffn2_w_gate), bf(ffn2_w_up), bf(ffn2_w_down)
    w_out_b = bf(w_out)
    wglu_b = bf(s5_w_glu)

    s5_steps = int(math.log2(min(S5_ROWS, seq // S5_CHUNK)))
    s5_mats = jax.vmap(functools.partial(_s5_matrices, nsteps=s5_steps))(
        s5_lambda_re, s5_lambda_im, s5_b_re, s5_b_im, s5_c_re, s5_c_im, s5_d, s5_log_step)

    gdn_conv_f = gdn_conv_w.astype(F32)

    h = x.reshape(t, d_model).astype(F32)
    for l in range(depth):
        h = _ffn_ln(h, f1g, f1u, f1d, _row3(ln1_g), _row3(ln1_b), l, alpha)
        proj, u8 = _in_proj(h, w_in_p, n_proj, s5_w, u_blk, l)
        y_s5 = _s5_core(u8, s5_mats, l, seq)
        y_gdn = _gdn_mix(proj, blk, gdn_conv_f, alog_l, dtb_l, _row3(gdn_norm_g), l, bsz, seq, n_heads)
        y_lru = _lru_mix(proj, lx_blk, lg_blk, lru_conv_w.astype(F32), _row3(lru_conv_b), lru_wh, lru_wl, lru_bias,
                         _row3(lru_lambda), l, bsz, seq)
        h = _out_proj_ln(h, y_s5, y_gdn, y_lru, wglu_b, _row3(s5_b_glu), w_out_b, _row3(ln2_g), _row3(ln2_b), l,
                         alpha)
        h = _ffn_ln(h, f2g, f2u, f2d, _row3(ln3_g), _row3(ln3_b), l, alpha)
    return h.reshape(bsz, seq, d_model).astype(x.dtype)
```
